```python
import math
import jax, jax.numpy as jnp
from jax import lax
import numpy as np

D_MODEL = 1024
BATCH = 4
SEQ = 8192
DEPTH = 2

N_MIXERS = 2
N_HGRN_LAYERS = (DEPTH + 1) // 2
N_SSD_LAYERS = DEPTH // 2
EPS = 1e-6

HG_EXPAND = 128
HG_HEADS = D_MODEL // HG_EXPAND
HG_DK = HG_EXPAND
HG_DV = D_MODEL // HG_HEADS
HG_QK = HG_HEADS * HG_DK
HG_V = HG_HEADS * HG_DV
HG_IN = 2 * HG_QK + 2 * HG_V
HG_CHUNK = 64

SSD_EXPAND = 2
D_INNER = SSD_EXPAND * D_MODEL
SSD_HEADDIM = 64
SSD_HEADS = D_INNER // SSD_HEADDIM
D_STATE = 128
N_GROUPS = 4
HEADS_PER_GROUP = SSD_HEADS // N_GROUPS
CONV_W = 4
CONV_DIM = D_INNER + 2 * N_GROUPS * D_STATE
SSD_IN = D_INNER + CONV_DIM + SSD_HEADS
SSD_CHUNK = 64
DT_MIN = 1e-3
DT_MAX = 1e-1

D_FF = 2816
N_EXPERTS = 8
TOP_K = 2

kernel_name = "hybrid_hgrn2_mamba2_moe_trunk"


def rmsnorm(x, w):
    xf = x.astype(jnp.float32)
    y = xf * lax.rsqrt(jnp.mean(xf * xf, axis=-1, keepdims=True) + EPS)
    return (y * w.astype(jnp.float32)).astype(x.dtype)


def to_chunks(t, c):
    b, l = t.shape[:2]
    t = t.reshape((b, l // c, c) + t.shape[2:])
    return jnp.moveaxis(t, 1, 0)


def from_chunks(t):
    t = jnp.moveaxis(t, 0, 1)
    return t.reshape((t.shape[0], t.shape[1] * t.shape[2]) + t.shape[3:])


def hgrn_lower_bounds(lb_logits):
    p = jax.nn.softmax(lb_logits.astype(jnp.float32), axis=0)
    return jnp.cumsum(p, axis=0)[:-1]


def hgrn2_mixer(h, w_in, lb, norm_w, w_out):
    b, l, _ = h.shape
    f32 = jnp.float32
    proj = (h @ w_in).astype(f32)
    q, f_pre, v, g = jnp.split(proj, [HG_QK, 2 * HG_QK, 2 * HG_QK + HG_V], axis=-1)
    f = lb + (1.0 - lb) * jax.nn.sigmoid(f_pre)
    log_f = jnp.log(f)
    k = 1.0 - f
    qh = to_chunks(q.reshape(b, l, HG_HEADS, HG_DK), HG_CHUNK)
    kh = to_chunks(k.reshape(b, l, HG_HEADS, HG_DK), HG_CHUNK)
    lfh = to_chunks(log_f.reshape(b, l, HG_HEADS, HG_DK), HG_CHUNK)
    vh = to_chunks(v.reshape(b, l, HG_HEADS, HG_DV), HG_CHUNK)
    causal = jnp.tril(jnp.ones((HG_CHUNK, HG_CHUNK), bool))[None, :, :, None, None]

    def step(state, inp):
        qc, kc, lfc, vc = inp
        bcum = jnp.cumsum(lfc, axis=1)
        diff = bcum[:, :, None] - bcum[:, None, :]
        decay = jnp.exp(jnp.where(causal, diff, -jnp.inf))
        attn = jnp.einsum('bthk,bshk,btshk->bhts', qc, kc, decay)
        o_intra = jnp.einsum('bhts,bshv->bthv', attn, vc)
        o_inter = jnp.einsum('bthk,bhkv->bthv', qc * jnp.exp(bcum), state)
        b_last = bcum[:, -1]
        k_dec = kc * jnp.exp(b_last[:, None] - bcum)
        new_state = state * jnp.exp(b_last)[..., None] + jnp.einsum('bshk,bshv->bhkv', k_dec, vc)
        return new_state, o_intra + o_inter

    s0 = jnp.zeros((b, HG_HEADS, HG_DK, HG_DV), f32)
    _, o = lax.scan(step, s0, (qh, kh, lfh, vh))
    o = from_chunks(o)
    o = rmsnorm(o, norm_w) * jax.nn.sigmoid(g).reshape(b, l, HG_HEADS, HG_DV)
    return o.reshape(b, l, HG_V).astype(h.dtype) @ w_out


def ssd_mixer(h, w_in, conv_w, conv_b, dt_bias, a_log, d_skip, norm_w, w_out):
    b, l, _ = h.shape
    f32 = jnp.float32
    proj = h @ w_in
    z, xbc, dt = jnp.split(proj, [D_INNER, D_INNER + CONV_DIM], axis=-1)
    xbc = lax.conv_general_dilated(
        xbc, conv_w[:, None, :].astype(xbc.dtype), window_strides=(1,),
        padding=((CONV_W - 1, 0),), dimension_numbers=('NWC', 'WIO', 'NWC'),
        feature_group_count=CONV_DIM)
    xbc = jax.nn.silu((xbc + conv_b.astype(xbc.dtype)).astype(f32))
    xs, bm, cm = jnp.split(xbc, [D_INNER, D_INNER + N_GROUPS * D_STATE], axis=-1)
    xs = xs.reshape(b, l, N_GROUPS, HEADS_PER_GROUP, SSD_HEADDIM)
    bm = bm.reshape(b, l, N_GROUPS, D_STATE)
    cm = cm.reshape(b, l, N_GROUPS, D_STATE)
    dt = jax.nn.softplus(dt.astype(f32) + dt_bias.astype(f32))
    dt = dt.reshape(b, l, N_GROUPS, HEADS_PER_GROUP)
    a = -jnp.exp(a_log.astype(f32)).reshape(N_GROUPS, HEADS_PER_GROUP)
    da = dt * a
    causal = jnp.tril(jnp.ones((SSD_CHUNK, SSD_CHUNK), bool))[None, :, :, None, None]

    def step(state, inp):
        xc, bc, cc, dtc, dac = inp
        cs = jnp.cumsum(dac, axis=1)
        seg = cs[:, :, None] - cs[:, None, :]
        lmat = jnp.exp(jnp.where(causal, seg, -jnp.inf))
        cb = jnp.einsum('btgn,bsgn->bgts', cc, bc)
        y_intra = jnp.einsum('bgts,btsgr,bsgr,bsgrp->btgrp', cb, lmat, dtc, xc)
        y_inter = jnp.einsum('btgn,bgrpn->btgrp', cc, state) * jnp.exp(cs)[..., None]
        last = cs[:, -1]
        w = dtc * jnp.exp(last[:, None] - cs)
        new_state = state * jnp.exp(last)[..., None, None] + jnp.einsum('bsgn,bsgr,bsgrp->bgrpn', bc, w, xc)
        return new_state, y_intra + y_inter

    s0 = jnp.zeros((b, N_GROUPS, HEADS_PER_GROUP, SSD_HEADDIM, D_STATE), f32)
    inputs = tuple(to_chunks(t, SSD_CHUNK) for t in (xs, bm, cm, dt, da))
    _, y = lax.scan(step, s0, inputs)
    y = from_chunks(y) + d_skip.astype(f32).reshape(N_GROUPS, HEADS_PER_GROUP)[..., None] * xs
    y = y.reshape(b, l, D_INNER) * jax.nn.silu(z.astype(f32))
    y = rmsnorm(y.reshape(b, l, N_GROUPS, D_INNER // N_GROUPS), jnp.ones((), f32))
    y = y.reshape(b, l, D_INNER) * norm_w.astype(f32)
    return y.astype(h.dtype) @ w_out


def swiglu(h, w_gate, w_up, w_down):
    return (jax.nn.silu(h @ w_gate) * (h @ w_up)) @ w_down


def moe_ffn(h, router_w, w_gate, w_up, w_down):
    b, l, d = h.shape
    t = h.reshape(b * l, d)
    probs = jax.nn.softmax((t @ router_w).astype(jnp.float32), axis=-1)
    top_p, top_i = lax.top_k(probs, TOP_K)
    top_p = top_p / jnp.sum(top_p, axis=-1, keepdims=True)
    gates = jnp.sum(jax.nn.one_hot(top_i, N_EXPERTS, dtype=jnp.float32) * top_p[..., None], axis=1)
    gates = gates.astype(t.dtype)
    out = jnp.zeros_like(t)
    for e in range(N_EXPERTS):
        out = out + gates[:, e:e + 1] * swiglu(t, w_gate[e], w_up[e], w_down[e])
    return out.reshape(b, l, d)


def setup_inputs(seed: int = 0) -> dict:
    key = jax.random.key(seed)
    ks = jax.random.split(key, 26)
    f32 = jnp.float32
    na, nb = N_HGRN_LAYERS, N_SSD_LAYERS

    def nrm(k, shape, fan_in):
        return jax.random.normal(k, shape, f32) * fan_in ** -0.5

    def gain(k, shape):
        return 1.0 + 0.02 * jax.random.normal(k, shape, f32)

    dt = jnp.exp(jax.random.uniform(ks[12], (nb, SSD_HEADS), f32, math.log(DT_MIN), math.log(DT_MAX)))
    return {
        "x": jax.random.normal(ks[0], (BATCH, SEQ, D_MODEL), f32),
        "mix_norm_w": gain(ks[1], (DEPTH, D_MODEL)),
        "ffn_norm_w": gain(ks[2], (DEPTH, D_MODEL)),
        "final_norm_w": gain(ks[3], (D_MODEL,)),
        "hg_w_in": nrm(ks[4], (na, D_MODEL, HG_IN), D_MODEL),
        "hg_lb_logits": 0.5 * jax.random.normal(ks[5], (na + 1, HG_QK), f32),
        "hg_norm_w": gain(ks[6], (na, HG_DV)),
        "hg_w_out": nrm(ks[7], (na, HG_V, D_MODEL), HG_V),
        "ssd_w_in": nrm(ks[8], (nb, D_MODEL, SSD_IN), D_MODEL),
        "ssd_conv_w": nrm(ks[9], (nb, CONV_W, CONV_DIM), CONV_W),
        "ssd_conv_b": 0.02 * jax.random.normal(ks[10], (nb, CONV_DIM), f32),
        "ssd_dt_bias": dt + jnp.log(-jnp.expm1(-dt)),
        "ssd_a_log": jnp.log(jax.random.uniform(ks[13], (nb, SSD_HEADS), f32, 1.0, 16.0)),
        "ssd_d": 1.0 + 0.1 * jax.random.normal(ks[14], (nb, SSD_HEADS), f32),
        "ssd_norm_w": gain(ks[15], (nb, D_INNER)),
        "ssd_w_out": nrm(ks[16], (nb, D_INNER, D_MODEL), D_INNER),
        "ffn_w_gate": nrm(ks[17], (na, D_MODEL, D_FF), D_MODEL),
        "ffn_w_up": nrm(ks[18], (na, D_MODEL, D_FF), D_MODEL),
        "ffn_w_down": nrm(ks[19], (na, D_FF, D_MODEL), D_FF),
        "moe_router": nrm(ks[20], (nb, D_MODEL, N_EXPERTS), D_MODEL),
        "moe_w_gate": nrm(ks[21], (nb, N_EXPERTS, D_MODEL, D_FF), D_MODEL),
        "moe_w_up": nrm(ks[22], (nb, N_EXPERTS, D_MODEL, D_FF), D_MODEL),
        "moe_w_down": nrm(ks[23], (nb, N_EXPERTS, D_FF, D_MODEL), D_FF),
    }


def reference(x, mix_norm_w, ffn_norm_w, final_norm_w,
              hg_w_in, hg_lb_logits, hg_norm_w, hg_w_out,
              ssd_w_in, ssd_conv_w, ssd_conv_b, ssd_dt_bias, ssd_a_log, ssd_d, ssd_norm_w, ssd_w_out,
              ffn_w_gate, ffn_w_up, ffn_w_down,
              moe_router, moe_w_gate, moe_w_up, moe_w_down):
    lbs = hgrn_lower_bounds(hg_lb_logits)
    for layer in range(DEPTH):
        j = layer // N_MIXERS
        h = rmsnorm(x, mix_norm_w[layer])
        if layer % N_MIXERS == 0:
            x = x + hgrn2_mixer(h, hg_w_in[j], lbs[j], hg_norm_w[j], hg_w_out[j])
        else:
            x = x + ssd_mixer(h, ssd_w_in[j], ssd_conv_w[j], ssd_conv_b[j], ssd_dt_bias[j],
                              ssd_a_log[j], ssd_d[j], ssd_norm_w[j], ssd_w_out[j])
        h = rmsnorm(x, ffn_norm_w[layer])
        if layer % 2 == 0:
            x = x + swiglu(h, ffn_w_gate[j], ffn_w_up[j], ffn_w_down[j])
        else:
            x = x + moe_ffn(h, moe_router[j], moe_w_gate[j], moe_w_up[j], moe_w_down[j])
    return rmsnorm(x, final_norm_w)
```

```python
import functools

import jax
import jax.numpy as jnp
from jax import lax
from jax.experimental import pallas as pl
from jax.experimental.pallas import tpu as pltpu

F32 = jnp.float32
BF16 = jnp.bfloat16
EPS = 1e-6
LANES = 128
VMEM_LIMIT = 60000 * 1024

HG_CHUNK = 64
HG_SUB = 16
HG_TOKENS = 256
SSD_CHUNK = 128
SSD_TOKENS = 256
SSD_HEADDIM = 64
SSD_STATE = 128
SSD_GROUPS = 4
CONV_W = 4
CONV_TAIL = 8
FFN_TOKENS = 512
FFN_CHUNK = 1408
MOE_EXPERTS = 8
MOE_TILE = 512
ROUTER_TOKENS = 512
COMBINE_TOKENS = 256


def _dot(a, b):
    return jnp.dot(a, b, preferred_element_type=F32)


def _dot_nt(a, b):
    return lax.dot_general(a, b, (((1,), (1,)), ((), ())), preferred_element_type=F32)


def _dot_tn(a, b):
    return lax.dot_general(a, b, (((0,), (0,)), ((), ())), preferred_element_type=F32)


def _sigmoid(x):
    return 1.0 / (1.0 + jnp.exp(-x))


def _rms_rows(x, w):
    return x * lax.rsqrt(jnp.mean(x * x, axis=-1, keepdims=True) + EPS) * w


def _tri_ones(n):
    r = lax.broadcasted_iota(jnp.int32, (n, n), 0)
    c = lax.broadcasted_iota(jnp.int32, (n, n), 1)
    return jnp.where(r >= c, 1.0, 0.0).astype(BF16)


def _cumsum_rows(tri, x):
    hi = x.astype(BF16)
    r1 = x - hi.astype(F32)
    mid = r1.astype(BF16)
    lo = (r1 - mid.astype(F32)).astype(BF16)
    return _dot(tri, hi) + _dot(tri, mid) + _dot(tri, lo)


def _hgrn_kernel(x_ref, mw_ref, win_ref, lbl_ref, nw_ref, wout_ref, o_ref, proj_s, o_s, st_s, *, nh, dk, dv):
    tc = x_ref.shape[1]
    qk = nh * dk
    vd = nh * dv
    nsub = HG_CHUNK // HG_SUB

    @pl.when(pl.program_id(1) == 0)
    def _():
        st_s[...] = jnp.zeros_like(st_s)

    x = x_ref[0]
    h = _rms_rows(x, mw_ref[...]).astype(BF16)
    proj_s[...] = _dot(h, win_ref[...])

    lbl = lbl_ref[...]
    lmax = jnp.max(lbl, axis=0, keepdims=True)
    le = jnp.exp(lbl - lmax)
    lb = le[0:1, :] / jnp.sum(le, axis=0, keepdims=True)

    tri = _tri_ones(HG_CHUNK)
    tio = lax.broadcasted_iota(jnp.int32, (HG_SUB, qk), 0)

    def chunk_body(c, carry):
        r0 = pl.multiple_of(c * HG_CHUNK, HG_CHUNK)
        q = proj_s[pl.ds(r0, HG_CHUNK), 0:qk]
        fp = proj_s[pl.ds(r0, HG_CHUNK), qk:2 * qk]
        v = proj_s[pl.ds(r0, HG_CHUNK), 2 * qk:2 * qk + vd]
        f = lb + (1.0 - lb) * _sigmoid(fp)
        k = 1.0 - f
        b = _cumsum_rows(tri, jnp.log(f))
        b_last = b[HG_CHUNK - 1:HG_CHUNK, :]
        qg = (q * jnp.exp(b)).astype(BF16)
        kdec = (k * jnp.exp(b_last - b)).astype(BF16)
        g_last = jnp.exp(b_last)
        vb = v.astype(BF16)

        o_sub = []
        for i in range(nsub):
            lo, hi = i * HG_SUB, (i + 1) * HG_SUB
            qi = q[lo:hi]
            bi = b[lo:hi]
            acc = [None] * nh
            if i > 0:
                ri = b[lo - 1:lo, :]
                qd = (qi * jnp.exp(bi - ri)).astype(BF16)
                kd = (k[:lo] * jnp.exp(ri - b[:lo])).astype(BF16)
                for hh in range(nh):
                    a = _dot_nt(qd[:, hh * dk:(hh + 1) * dk], kd[:, hh * dk:(hh + 1) * dk])
                    acc[hh] = _dot(a.astype(BF16), vb[:lo, hh * dv:(hh + 1) * dv])
            for s in range(HG_SUB):
                r = lo + s
                dec = jnp.exp(jnp.where(tio >= s, bi - b[r:r + 1, :], -jnp.inf))
                p = qi * k[r:r + 1, :] * dec
                for hh in range(nh):
                    a = jnp.sum(p[:, hh * dk:(hh + 1) * dk], axis=-1, keepdims=True)
                    t = a * v[r:r + 1, hh * dv:(hh + 1) * dv]
                    acc[hh] = t if acc[hh] is None else acc[hh] + t
            o_sub.append(acc)

        o_heads = []
        for hh in range(nh):
            st = st_s[hh]
            o_inter = _dot_nt(qg[:, hh * dk:(hh + 1) * dk], st.astype(BF16))
            o_heads.append(o_inter + jnp.concatenate([o_sub[i][hh] for i in range(nsub)], axis=0))
            st_s[hh] = st * g_last[:, hh * dk:(hh + 1) * dk] + _dot_tn(
                vb[:, hh * dv:(hh + 1) * dv], kdec[:, hh * dk:(hh + 1) * dk])
        o_s[pl.ds(r0, HG_CHUNK), :] = jnp.concatenate(o_heads, axis=1)
        return carry

    lax.fori_loop(0, tc // HG_CHUNK, chunk_body, 0)

    o = o_s[...]
    g = proj_s[:, 2 * qk + vd:2 * qk + 2 * vd]
    parts = []
    for hh in range(nh):
        oh = o[:, hh * dv:(hh + 1) * dv]
        parts.append(oh * lax.rsqrt(jnp.mean(oh * oh, axis=-1, keepdims=True) + EPS))
    on = jnp.concatenate(parts, axis=1) * nw_ref[...] * _sigmoid(g)
    o_ref[0] = x + _dot(on.astype(BF16), wout_ref[...])


def _hgrn_layer(x, mix_w, w_in, lb_logits, norm_w, w_out):
    bsz, seq, d = x.shape
    dv = norm_w.shape[0]
    nh = w_out.shape[0] // dv
    dk = (w_in.shape[1] - 2 * nh * dv) // (2 * nh)
    assert lb_logits.shape[0] == 2 and dk == LANES and dv == LANES
    tc = HG_TOKENS
    assert seq % tc == 0 and tc % HG_CHUNK == 0
    const = lambda b, j: (0, 0)
    return pl.pallas_call(
        functools.partial(_hgrn_kernel, nh=nh, dk=dk, dv=dv),
        grid=(bsz, seq // tc),
        in_specs=[
            pl.BlockSpec((1, tc, d), lambda b, j: (b, j, 0)),
            pl.BlockSpec((1, d), const),
            pl.BlockSpec(w_in.shape, const, pipeline_mode=pl.Buffered(1)),
            pl.BlockSpec(lb_logits.shape, const),
            pl.BlockSpec((1, nh * dv), const),
            pl.BlockSpec(w_out.shape, const, pipeline_mode=pl.Buffered(1)),
        ],
        out_specs=pl.BlockSpec((1, tc, d), lambda b, j: (b, j, 0)),
        out_shape=jax.ShapeDtypeStruct(x.shape, F32),
        scratch_shapes=[
            pltpu.VMEM((tc, w_in.shape[1]), F32),
            pltpu.VMEM((tc, nh * dv), F32),
            pltpu.VMEM((nh, dv, dk), F32),
        ],
        compiler_params=pltpu.CompilerParams(
            dimension_semantics=("arbitrary", "arbitrary"), vmem_limit_bytes=VMEM_LIMIT),
        name="hgrn_mixer",
    )(x, mix_w.reshape(1, d), w_in.astype(BF16), lb_logits, jnp.tile(norm_w, nh).reshape(1, nh * dv),
      w_out.astype(BF16))


def _swiglu_rows(h, wg_ref, wu_ref, wd_ref, acc):
    dff = wg_ref.shape[-1]
    for c0 in range(0, dff, FFN_CHUNK):
        g = _dot(h, wg_ref[:, c0:c0 + FFN_CHUNK])
        u = _dot(h, wu_ref[:, c0:c0 + FFN_CHUNK])
        a = (g * _sigmoid(g) * u).astype(BF16)
        acc = acc + _dot(a, wd_ref[c0:c0 + FFN_CHUNK, :])
    return acc


def _ffn_kernel(x_ref, nw_ref, wg_ref, wu_ref, wd_ref, o_ref):
    x = x_ref[...]
    h = _rms_rows(x, nw_ref[...]).astype(BF16)
    o_ref[...] = _swiglu_rows(h, wg_ref, wu_ref, wd_ref, x)


def _ffn_layer(x2, norm_w, w_gate, w_up, w_down):
    t, d = x2.shape
    dff = w_gate.shape[1]
    assert t % FFN_TOKENS == 0 and dff % FFN_CHUNK == 0
    const = lambda i: (0, 0)
    return pl.pallas_call(
        _ffn_kernel,
        grid=(t // FFN_TOKENS,),
        in_specs=[
            pl.BlockSpec((FFN_TOKENS, d), lambda i: (i, 0)),
            pl.BlockSpec((1, d), const),
            pl.BlockSpec((d, dff), const, pipeline_mode=pl.Buffered(1)),
            pl.BlockSpec((d, dff), const, pipeline_mode=pl.Buffered(1)),
            pl.BlockSpec((dff, d), const, pipeline_mode=pl.Buffered(1)),
        ],
        out_specs=pl.BlockSpec((FFN_TOKENS, d), lambda i: (i, 0)),
        out_shape=jax.ShapeDtypeStruct(x2.shape, F32),
        compiler_params=pltpu.CompilerParams(
            dimension_semantics=("arbitrary",), vmem_limit_bytes=VMEM_LIMIT),
        name="dense_ffn",
    )(x2, norm_w.reshape(1, d), w_gate.astype(BF16), w_up.astype(BF16), w_down.astype(BF16))


def _ssd_kernel(x_ref, mw_ref, win_ref, cw_ref, cb_ref, dtb_ref, alog_ref, dsk_ref, nw_ref, wout_ref, o_ref,
                proj_s, cbuf, y_s, st_s, *, d_inner):
    tc = x_ref.shape[1]
    p2 = 2 * SSD_HEADDIM
    assert p2 == LANES and SSD_STATE == LANES
    gs = SSD_GROUPS * SSD_STATE
    cd = d_inner + 2 * gs
    pairs_per_group = d_inner // (SSD_GROUPS * p2)
    q = SSD_CHUNK

    @pl.when(pl.program_id(1) == 0)
    def _():
        st_s[...] = jnp.zeros_like(st_s)
        cbuf[0:CONV_TAIL, :] = jnp.zeros((CONV_TAIL, cd), F32)

    x = x_ref[0]
    h = _rms_rows(x, mw_ref[...]).astype(BF16)
    proj_s[...] = _dot(h, win_ref[...])

    cbuf[CONV_TAIL:CONV_TAIL + tc, :] = proj_s[:, d_inner:d_inner + cd]
    conv = cb_ref[...]
    for j in range(CONV_W):
        off = CONV_TAIL - (CONV_W - 1) + j
        conv = conv + cw_ref[j:j + 1, :] * cbuf[off:off + tc, :]
    cbuf[0:CONV_TAIL, :] = cbuf[tc:tc + CONV_TAIL, :]
    proj_s[:, d_inner:d_inner + cd] = conv * _sigmoid(conv)

    dt_pre = proj_s[:, d_inner + cd:d_inner + cd + LANES] + dtb_ref[...]
    dt = jnp.maximum(dt_pre, 0.0) + jnp.log1p(jnp.exp(-jnp.abs(dt_pre)))
    da = dt * (-jnp.exp(alog_ref[...]))

    tri = _tri_ones(q)
    rr = lax.broadcasted_iota(jnp.int32, (q, q), 0)
    cc = lax.broadcasted_iota(jnp.int32, (q, q), 1)
    causal = rr >= cc
    first = lax.broadcasted_iota(jnp.int32, (1, LANES), 1) < SSD_HEADDIM

    for sc in range(tc // q):
        lo, hi = sc * q, (sc + 1) * q
        dtc = dt[lo:hi]
        cs = _cumsum_rows(tri, da[lo:hi])
        last = cs[q - 1:q, :]
        wgt = dtc * jnp.exp(last - cs)
        ecs = jnp.exp(cs)
        glast = jnp.exp(last)
        cs_t = cs.T
        dt_t = dtc.T
        for g in range(SSD_GROUPS):
            bm = proj_s[lo:hi, 2 * d_inner + g * SSD_STATE:2 * d_inner + (g + 1) * SSD_STATE]
            cm = proj_s[lo:hi, 2 * d_inner + gs + g * SSD_STATE:2 * d_inner + gs + (g + 1) * SSD_STATE]
            bmb = bm.astype(BF16)
            cmb = cm.astype(BF16)
            cb = _dot_nt(cmb, bmb)
            for pr in range(pairs_per_group):
                pi = g * pairs_per_group + pr
                h1, h2 = 2 * pi, 2 * pi + 1
                xp = proj_s[lo:hi, d_inner + pi * p2:d_inner + (pi + 1) * p2]
                m1 = cb * jnp.exp(jnp.where(causal, cs[:, h1:h1 + 1] - cs_t[h1:h1 + 1, :], -jnp.inf)) * dt_t[h1:h1 + 1, :]
                m2 = cb * jnp.exp(jnp.where(causal, cs[:, h2:h2 + 1] - cs_t[h2:h2 + 1, :], -jnp.inf)) * dt_t[h2:h2 + 1, :]
                x1 = jnp.where(first, xp, 0.0).astype(BF16)
                x2 = jnp.where(first, 0.0, xp).astype(BF16)
                y = _dot(m1.astype(BF16), x1) + _dot(m2.astype(BF16), x2)
                st = st_s[pi]
                y = y + jnp.where(first, ecs[:, h1:h1 + 1], ecs[:, h2:h2 + 1]) * _dot(cmb, st.astype(BF16))
                wx = (jnp.where(first, wgt[:, h1:h1 + 1], wgt[:, h2:h2 + 1]) * xp).astype(BF16)
                st_s[pi] = st * jnp.where(first, glast[:, h1:h1 + 1], glast[:, h2:h2 + 1]) + _dot_tn(bmb, wx)
                y_s[lo:hi, pi * p2:(pi + 1) * p2] = y + dsk_ref[:, pi * p2:(pi + 1) * p2] * xp

    z = proj_s[:, 0:d_inner]
    y = y_s[...] * (z * _sigmoid(z))
    gw = d_inner // SSD_GROUPS
    parts = []
    for g in range(SSD_GROUPS):
        yg = y[:, g * gw:(g + 1) * gw]
        parts.append(yg * lax.rsqrt(jnp.mean(yg * yg, axis=-1, keepdims=True) + EPS))
    yn = jnp.concatenate(parts, axis=1) * nw_ref[...]
    o_ref[0] = x + _dot(yn.astype(BF16), wout_ref[...])


def _ssd_layer(x, mix_w, w_in, conv_w, conv_b, dt_bias, a_log, d_skip, norm_w, w_out):
    bsz, seq, d = x.shape
    d_inner = w_out.shape[0]
    nheads = dt_bias.shape[0]
    assert d_inner == nheads * SSD_HEADDIM and nheads <= LANES
    cd = d_inner + 2 * SSD_GROUPS * SSD_STATE
    assert conv_w.shape == (CONV_W, cd) and w_in.shape[1] == d_inner + cd + nheads
    tc = SSD_TOKENS
    assert seq % tc == 0 and tc % SSD_CHUNK == 0
    pad = LANES - nheads
    w_all = jnp.pad(w_in, ((0, 0), (0, pad))).astype(BF16)
    n_all = w_all.shape[1]
    const = lambda b, j: (0, 0)
    return pl.pallas_call(
        functools.partial(_ssd_kernel, d_inner=d_inner),
        grid=(bsz, seq // tc),
        in_specs=[
            pl.BlockSpec((1, tc, d), lambda b, j: (b, j, 0)),
            pl.BlockSpec((1, d), const),
            pl.BlockSpec(w_all.shape, const, pipeline_mode=pl.Buffered(1)),
            pl.BlockSpec((CONV_W, cd), const),
            pl.BlockSpec((1, cd), const),
            pl.BlockSpec((1, LANES), const),
            pl.BlockSpec((1, LANES), const),
            pl.BlockSpec((1, d_inner), const),
            pl.BlockSpec((1, d_inner), const),
            pl.BlockSpec(w_out.shape, const, pipeline_mode=pl.Buffered(1)),
        ],
        out_specs=pl.BlockSpec((1, tc, d), lambda b, j: (b, j, 0)),
        out_shape=jax.ShapeDtypeStruct(x.shape, F32),
        scratch_shapes=[
            pltpu.VMEM((tc, n_all), F32),
            pltpu.VMEM((tc + CONV_TAIL, cd), F32),
            pltpu.VMEM((tc, d_inner), F32),
            pltpu.VMEM((nheads // 2, SSD_STATE, LANES), F32),
        ],
        compiler_params=pltpu.CompilerParams(
            dimension_semantics=("arbitrary", "arbitrary"), vmem_limit_bytes=VMEM_LIMIT),
        name="ssd_mixer",
    )(x, mix_w.reshape(1, d), w_all, conv_w, conv_b.reshape(1, cd),
      jnp.pad(dt_bias, (0, pad)).reshape(1, LANES), jnp.pad(a_log, (0, pad)).reshape(1, LANES),
      jnp.repeat(d_skip, SSD_HEADDIM).reshape(1, d_inner), norm_w.reshape(1, d_inner), w_out.astype(BF16))


def _router_kernel(x_ref, nw_ref, rw_ref, idx_ref, gate_ref):
    h = _rms_rows(x_ref[...], nw_ref[...])
    logits = jnp.dot(h, rw_ref[...], precision=lax.Precision.HIGHEST, preferred_element_type=F32)
    lane = lax.broadcasted_iota(jnp.int32, logits.shape, 1)
    valid = lane < MOE_EXPERTS
    lg = jnp.where(valid, logits, -jnp.inf)
    e = jnp.exp(lg - jnp.max(lg, axis=-1, keepdims=True))
    p = jnp.where(valid, e / jnp.sum(e, axis=-1, keepdims=True), -1.0)
    m1 = jnp.max(p, axis=-1, keepdims=True)
    i1 = jnp.min(jnp.where(p == m1, lane, LANES), axis=-1, keepdims=True)
    p2 = jnp.where(lane == i1, -1.0, p)
    m2 = jnp.max(p2, axis=-1, keepdims=True)
    i2 = jnp.min(jnp.where(p2 == m2, lane, LANES), axis=-1, keepdims=True)
    s = m1 + m2
    idx_ref[...] = jnp.where(lane == 0, i1, jnp.where(lane == 1, i2, 0))
    gate_ref[...] = jnp.where(lane == 0, m1 / s, jnp.where(lane == 1, m2 / s, 0.0))


def _router(x2, norm_w, router_w):
    t, d = x2.shape
    ne = router_w.shape[1]
    assert ne == MOE_EXPERTS and t % ROUTER_TOKENS == 0
    rw = jnp.pad(router_w, ((0, 0), (0, LANES - ne)))
    return pl.pallas_call(
        _router_kernel,
        grid=(t // ROUTER_TOKENS,),
        in_specs=[
            pl.BlockSpec((ROUTER_TOKENS, d), lambda i: (i, 0)),
            pl.BlockSpec((1, d), lambda i: (0, 0)),
            pl.BlockSpec((d, LANES), lambda i: (0, 0)),
        ],
        out_specs=[pl.BlockSpec((ROUTER_TOKENS, LANES), lambda i: (i, 0))] * 2,
        out_shape=[jax.ShapeDtypeStruct((t, LANES), jnp.int32), jax.ShapeDtypeStruct((t, LANES), F32)],
        compiler_params=pltpu.CompilerParams(dimension_semantics=("arbitrary",)),
        name="moe_router",
    )(x2, norm_w.reshape(1, d), rw)


def _row_gather_start(src_hbm, dst, row_of, n, sem):
    def issue(r, carry):
        pltpu.make_async_copy(src_hbm.at[pl.ds(row_of(r), 1), :], dst.at[pl.ds(r, 1), :], sem).start()
        return carry
    lax.fori_loop(0, n, issue, 0, unroll=8)


def _row_gather_wait(src_hbm, dst, n, sem):
    pltpu.make_async_copy(src_hbm.at[pl.ds(0, n), :], dst, sem).wait()


def _experts_kernel(te_ref, tv_ref, rows_ref, gate_ref, nw_ref, x_hbm, wg_ref, wu_ref, wd_ref, y_ref, xbuf, sem):
    i = pl.program_id(0)
    tm = xbuf.shape[0]

    @pl.when(tv_ref[i] > 0)
    def _():
        _row_gather_start(x_hbm, xbuf, lambda r: rows_ref[0, 0, r], tm, sem)
        _row_gather_wait(x_hbm, xbuf, tm, sem)
        h = _rms_rows(xbuf[...], nw_ref[...]).astype(BF16)
        acc = _swiglu_rows(h, wg_ref.at[0], wu_ref.at[0], wd_ref.at[0], jnp.zeros(y_ref.shape, F32))
        y_ref[...] = acc * gate_ref[...]

    @pl.when(tv_ref[i] == 0)
    def _():
        y_ref[...] = jnp.zeros_like(y_ref)


def _experts(x2, norm_w, tile_expert, tile_valid, row_token, row_gate, w_gate, w_up, w_down):
    t, d = x2.shape
    ne, _, dff = w_gate.shape
    nt = tile_expert.shape[0]
    tm = MOE_TILE
    wmap = lambda i, te, tv: (te[i], 0, 0)
    grid_spec = pltpu.PrefetchScalarGridSpec(
        num_scalar_prefetch=2,
        grid=(nt,),
        in_specs=[
            pl.BlockSpec((1, 1, tm), lambda i, te, tv: (i, 0, 0), memory_space=pltpu.SMEM),
            pl.BlockSpec((tm, 1), lambda i, te, tv: (i, 0)),
            pl.BlockSpec((1, d), lambda i, te, tv: (0, 0)),
            pl.BlockSpec(memory_space=pl.ANY),
            pl.BlockSpec((1, d, dff), wmap),
            pl.BlockSpec((1, d, dff), wmap),
            pl.BlockSpec((1, dff, d), wmap),
        ],
        out_specs=pl.BlockSpec((tm, d), lambda i, te, tv: (i, 0)),
        scratch_shapes=[pltpu.VMEM((tm, d), F32), pltpu.SemaphoreType.DMA(())],
    )
    return pl.pallas_call(
        _experts_kernel,
        grid_spec=grid_spec,
        out_shape=jax.ShapeDtypeStruct((nt * tm, d), F32),
        compiler_params=pltpu.CompilerParams(
            dimension_semantics=("arbitrary",), vmem_limit_bytes=VMEM_LIMIT),
        name="moe_experts",
    )(tile_expert, tile_valid, row_token.reshape(nt, 1, tm), row_gate.reshape(nt * tm, 1), norm_w.reshape(1, d),
      x2, w_gate.astype(BF16), w_up.astype(BF16), w_down.astype(BF16))


def _combine_kernel(pos_ref, x_ref, fw_ref, y_hbm, o_ref, ybuf, sem):
    tb = x_ref.shape[0]
    for j in range(2):
        _row_gather_start(y_hbm, ybuf.at[j], lambda r, j=j: pos_ref[0, 0, 2 * r + j], tb, sem.at[j])
    for j in range(2):
        _row_gather_wait(y_hbm, ybuf.at[j], tb, sem.at[j])
    o_ref[...] = _rms_rows(x_ref[...] + ybuf[0] + ybuf[1], fw_ref[...])


def _combine(x2, final_w, pos, y_rows):
    t, d = x2.shape
    tb = COMBINE_TOKENS
    assert t % tb == 0
    return pl.pallas_call(
        _combine_kernel,
        grid=(t // tb,),
        in_specs=[
            pl.BlockSpec((1, 1, 2 * tb), lambda i: (i, 0, 0), memory_space=pltpu.SMEM),
            pl.BlockSpec((tb, d), lambda i: (i, 0)),
            pl.BlockSpec((1, d), lambda i: (0, 0)),
            pl.BlockSpec(memory_space=pl.ANY),
        ],
        out_specs=pl.BlockSpec((tb, d), lambda i: (i, 0)),
        out_shape=jax.ShapeDtypeStruct(x2.shape, F32),
        scratch_shapes=[pltpu.VMEM((2, tb, d), F32), pltpu.SemaphoreType.DMA((2,))],
        compiler_params=pltpu.CompilerParams(dimension_semantics=("arbitrary",)),
        name="moe_combine",
    )(pos.reshape(t // tb, 1, 2 * tb), x2, final_w.reshape(1, d), y_rows)


def _moe_layer_and_final_norm(x2, norm_w, final_w, router_w, w_gate, w_up, w_down):
    t, d = x2.shape
    ne = MOE_EXPERTS
    tm = MOE_TILE
    idx_pad, gate_pad = _router(x2, norm_w, router_w)
    flat_e = idx_pad[:, :2].reshape(-1)
    flat_g = gate_pad[:, :2].reshape(-1)
    onehot = (flat_e[:, None] == jnp.arange(ne, dtype=jnp.int32)[None, :]).astype(jnp.int32)
    csum = jnp.cumsum(onehot, axis=0)
    rank = jnp.take_along_axis(csum, flat_e[:, None], axis=1)[:, 0] - 1
    counts = csum[-1]
    padded = ((counts + tm - 1) // tm) * tm
    ends = jnp.cumsum(padded)
    starts = ends - padded
    pos = (starts[flat_e] + rank).astype(jnp.int32)
    nt = (2 * t) // tm + ne
    row_token = jnp.zeros((nt * tm,), jnp.int32).at[pos].set(jnp.arange(2 * t, dtype=jnp.int32) // 2)
    row_gate = jnp.zeros((nt * tm,), F32).at[pos].set(flat_g)
    tile_start = jnp.arange(nt, dtype=jnp.int32) * tm
    tile_valid = (tile_start < ends[-1]).astype(jnp.int32)
    tile_expert = jnp.minimum(jnp.searchsorted(ends, tile_start, side="right"), ne - 1).astype(jnp.int32)
    y_rows = _experts(x2, norm_w, tile_expert, tile_valid, row_token, row_gate, w_gate, w_up, w_down)
    return _combine(x2, final_w, pos, y_rows)


def kernel(x, mix_norm_w, ffn_norm_w, final_norm_w, hg_w_in, hg_lb_logits, hg_norm_w, hg_w_out, ssd_w_in, ssd_conv_w, ssd_conv_b, ssd_dt_bias, ssd_a_log, ssd_d, ssd_norm_w, ssd_w_out, ffn_w_gate, ffn_w_up, ffn_w_down, moe_router, moe_w_gate, moe_w_up, moe_w_down):
    bsz, seq, d = x.shape
    assert mix_norm_w.shape[0] == 2 and hg_w_in.shape[0] == 1 and ssd_w_in.shape[0] == 1
    x = _hgrn_layer(x, mix_norm_w[0], hg_w_in[0], hg_lb_logits, hg_norm_w[0], hg_w_out[0])
    x2 = _ffn_layer(x.reshape(bsz * seq, d), ffn_norm_w[0], ffn_w_gate[0], ffn_w_up[0], ffn_w_down[0])
    x = _ssd_layer(x2.reshape(bsz, seq, d), mix_norm_w[1], ssd_w_in[0], ssd_conv_w[0], ssd_conv_b[0],
                   ssd_dt_bias[0], ssd_a_log[0], ssd_d[0], ssd_norm_w[0], ssd_w_out[0])
    out = _moe_layer_and_final_norm(x.reshape(bsz * seq, d), ffn_norm_w[1], final_norm_w, moe_router[0],
                                    moe_w_gate[0], moe_w_up[0], moe_w_down[0])
    return out.reshape(bsz, seq, d)
```

```python
import functools

import jax
import jax.numpy as jnp
from jax import lax
from jax.experimental import pallas as pl
from jax.experimental.pallas import tpu as pltpu

F32 = jnp.float32
BF16 = jnp.bfloat16
EPS = 1e-6
LANES = 128
VMEM_LIMIT = 60000 * 1024

HG_CHUNK = 64
HG_SUB = 16
HG_TOKENS = 256
SSD_CHUNK = 128
SSD_TOKENS = 256
SSD_HEADDIM = 64
SSD_STATE = 128
SSD_GROUPS = 4
CONV_W = 4
CONV_TAIL = 8
FFN_TOKENS = 512
FFN_CHUNK = 1408
MOE_EXPERTS = 8
MOE_TILE = 512
ROUTER_TOKENS = 512
COMBINE_TOKENS = 256


def _dot(a, b):
    return jnp.dot(a, b, preferred_element_type=F32)


def _dot_nt(a, b):
    return lax.dot_general(a, b, (((1,), (1,)), ((), ())), preferred_element_type=F32)


def _dot_tn(a, b):
    return lax.dot_general(a, b, (((0,), (0,)), ((), ())), preferred_element_type=F32)


def _sigmoid(x):
    return 1.0 / (1.0 + jnp.exp(-x))


def _rms_rows(x, w):
    return x * lax.rsqrt(jnp.mean(x * x, axis=-1, keepdims=True) + EPS) * w


def _tri_ones(n):
    r = lax.broadcasted_iota(jnp.int32, (n, n), 0)
    c = lax.broadcasted_iota(jnp.int32, (n, n), 1)
    return jnp.where(r >= c, 1.0, 0.0).astype(BF16)


def _cumsum_rows(tri, x):
    hi = x.astype(BF16)
    r1 = x - hi.astype(F32)
    mid = r1.astype(BF16)
    lo = (r1 - mid.astype(F32)).astype(BF16)
    return _dot(tri, hi) + _dot(tri, mid) + _dot(tri, lo)


def _hgrn_kernel(x_ref, mw_ref, win_ref, lbl_ref, nw_ref, wout_ref, o_ref, proj_s, o_s, st_s, *, nh, dk, dv):
    tc = x_ref.shape[1]
    qk = nh * dk
    vd = nh * dv
    nsub = HG_CHUNK // HG_SUB

    @pl.when(pl.program_id(1) == 0)
    def _():
        st_s[...] = jnp.zeros_like(st_s)

    x = x_ref[0]
    h = _rms_rows(x, mw_ref[...]).astype(BF16)
    proj_s[...] = _dot(h, win_ref[...])

    lbl = lbl_ref[...]
    lmax = jnp.max(lbl, axis=0, keepdims=True)
    le = jnp.exp(lbl - lmax)
    lb = le[0:1, :] / jnp.sum(le, axis=0, keepdims=True)

    tri = _tri_ones(HG_CHUNK)
    tio = lax.broadcasted_iota(jnp.int32, (HG_SUB, qk), 0)

    def chunk_body(c, carry):
        r0 = pl.multiple_of(c * HG_CHUNK, HG_CHUNK)
        q = proj_s[pl.ds(r0, HG_CHUNK), 0:qk]
        fp = proj_s[pl.ds(r0, HG_CHUNK), qk:2 * qk]
        v = proj_s[pl.ds(r0, HG_CHUNK), 2 * qk:2 * qk + vd]
        f = lb + (1.0 - lb) * _sigmoid(fp)
        k = 1.0 - f
        b = _cumsum_rows(tri, jnp.log(f))
        b_last = b[HG_CHUNK - 1:HG_CHUNK, :]
        qg = (q * jnp.exp(b)).astype(BF16)
        kdec = (k * jnp.exp(b_last - b)).astype(BF16)
        g_last = jnp.exp(b_last)
        vb = v.astype(BF16)

        o_sub = []
        for i in range(nsub):
            lo, hi = i * HG_SUB, (i + 1) * HG_SUB
            qi = q[lo:hi]
            bi = b[lo:hi]
            acc = [None] * nh
            if i > 0:
                ri = b[lo - 1:lo, :]
                qd = (qi * jnp.exp(bi - ri)).astype(BF16)
                kd = (k[:lo] * jnp.exp(ri - b[:lo])).astype(BF16)
                for hh in range(nh):
                    a = _dot_nt(qd[:, hh * dk:(hh + 1) * dk], kd[:, hh * dk:(hh + 1) * dk])
                    acc[hh] = _dot(a.astype(BF16), vb[:lo, hh * dv:(hh + 1) * dv])
            for s in range(HG_SUB):
                r = lo + s
                dec = jnp.exp(jnp.where(tio >= s, bi - b[r:r + 1, :], -jnp.inf))
                p = qi * k[r:r + 1, :] * dec
                for hh in range(nh):
                    a = jnp.sum(p[:, hh * dk:(hh + 1) * dk], axis=-1, keepdims=True)
                    t = a * v[r:r + 1, hh * dv:(hh + 1) * dv]
                    acc[hh] = t if acc[hh] is None else acc[hh] + t
            o_sub.append(acc)

        o_heads = []
        for hh in range(nh):
            st = st_s[hh]
            o_inter = _dot_nt(qg[:, hh * dk:(hh + 1) * dk], st.astype(BF16))
            o_heads.append(o_inter + jnp.concatenate([o_sub[i][hh] for i in range(nsub)], axis=0))
            st_s[hh] = st * g_last[:, hh * dk:(hh + 1) * dk] + _dot_tn(
                vb[:, hh * dv:(hh + 1) * dv], kdec[:, hh * dk:(hh + 1) * dk])
        o_s[pl.ds(r0, HG_CHUNK), :] = jnp.concatenate(o_heads, axis=1)
        return carry

    lax.fori_loop(0, tc // HG_CHUNK, chunk_body, 0)

    o = o_s[...]
    g = proj_s[:, 2 * qk + vd:2 * qk + 2 * vd]
    parts = []
    for hh in range(nh):
        oh = o[:, hh * dv:(hh + 1) * dv]
        parts.append(oh * lax.rsqrt(jnp.mean(oh * oh, axis=-1, keepdims=True) + EPS))
    on = jnp.concatenate(parts, axis=1) * nw_ref[...] * _sigmoid(g)
    o_ref[0] = x + _dot(on.astype(BF16), wout_ref[...])


def _hgrn_layer(x, mix_w, w_in, lb_logits, norm_w, w_out):
    bsz, seq, d = x.shape
    dv = norm_w.shape[0]
    nh = w_out.shape[0] // dv
    dk = (w_in.shape[1] - 2 * nh * dv) // (2 * nh)
    assert lb_logits.shape[0] == 2 and dk == LANES and dv == LANES
    tc = HG_TOKENS
    assert seq % tc == 0 and tc % HG_CHUNK == 0
    const = lambda b, j: (0, 0)
    return pl.pallas_call(
        functools.partial(_hgrn_kernel, nh=nh, dk=dk, dv=dv),
        grid=(bsz, seq // tc),
        in_specs=[
            pl.BlockSpec((1, tc, d), lambda b, j: (b, j, 0)),
            pl.BlockSpec((1, d), const),
            pl.BlockSpec(w_in.shape, const, pipeline_mode=pl.Buffered(1)),
            pl.BlockSpec(lb_logits.shape, const),
            pl.BlockSpec((1, nh * dv), const),
            pl.BlockSpec(w_out.shape, const, pipeline_mode=pl.Buffered(1)),
        ],
        out_specs=pl.BlockSpec((1, tc, d), lambda b, j: (b, j, 0)),
        out_shape=jax.ShapeDtypeStruct(x.shape, F32),
        scratch_shapes=[
            pltpu.VMEM((tc, w_in.shape[1]), F32),
            pltpu.VMEM((tc, nh * dv), F32),
            pltpu.VMEM((nh, dv, dk), F32),
        ],
        compiler_params=pltpu.CompilerParams(
            dimension_semantics=("arbitrary", "arbitrary"), vmem_limit_bytes=VMEM_LIMIT),
        name="hgrn_mixer",
    )(x, mix_w.reshape(1, d), w_in.astype(BF16), lb_logits, jnp.tile(norm_w, nh).reshape(1, nh * dv),
      w_out.astype(BF16))


def _swiglu_rows(h, wg_ref, wu_ref, wd_ref, acc):
    dff = wg_ref.shape[-1]
    for c0 in range(0, dff, FFN_CHUNK):
        g = _dot(h, wg_ref[:, c0:c0 + FFN_CHUNK])
        u = _dot(h, wu_ref[:, c0:c0 + FFN_CHUNK])
        a = (g * _sigmoid(g) * u).astype(BF16)
        y = _dot(a, wd_ref[c0:c0 + FFN_CHUNK, :])
        acc = y if acc is None else acc + y
    return acc


def _ffn_kernel(x_ref, nw_ref, wg_ref, wu_ref, wd_ref, o_ref):
    x = x_ref[...]
    h = _rms_rows(x, nw_ref[...]).astype(BF16)
    o_ref[...] = _swiglu_rows(h, wg_ref, wu_ref, wd_ref, x)


def _ffn_layer(x2, norm_w, w_gate, w_up, w_down):
    t, d = x2.shape
    dff = w_gate.shape[1]
    assert t % FFN_TOKENS == 0 and dff % FFN_CHUNK == 0
    const = lambda i: (0, 0)
    return pl.pallas_call(
        _ffn_kernel,
        grid=(t // FFN_TOKENS,),
        in_specs=[
            pl.BlockSpec((FFN_TOKENS, d), lambda i: (i, 0)),
            pl.BlockSpec((1, d), const),
            pl.BlockSpec((d, dff), const, pipeline_mode=pl.Buffered(1)),
            pl.BlockSpec((d, dff), const, pipeline_mode=pl.Buffered(1)),
            pl.BlockSpec((dff, d), const, pipeline_mode=pl.Buffered(1)),
        ],
        out_specs=pl.BlockSpec((FFN_TOKENS, d), lambda i: (i, 0)),
        out_shape=jax.ShapeDtypeStruct(x2.shape, F32),
        compiler_params=pltpu.CompilerParams(
            dimension_semantics=("arbitrary",), vmem_limit_bytes=VMEM_LIMIT),
        name="dense_ffn",
    )(x2, norm_w.reshape(1, d), w_gate.astype(BF16), w_up.astype(BF16), w_down.astype(BF16))


def _ssd_kernel(x_ref, mw_ref, win_ref, cw_ref, cb_ref, dtb_ref, alog_ref, dsk_ref, nw_ref, wout_ref, o_ref,
                proj_s, cbuf, y_s, st_s, *, d_inner):
    tc = x_ref.shape[1]
    p2 = 2 * SSD_HEADDIM
    assert p2 == LANES and SSD_STATE == LANES
    gs = SSD_GROUPS * SSD_STATE
    cd = d_inner + 2 * gs
    pairs_per_group = d_inner // (SSD_GROUPS * p2)
    q = SSD_CHUNK

    @pl.when(pl.program_id(1) == 0)
    def _():
        st_s[...] = jnp.zeros_like(st_s)
        cbuf[0:CONV_TAIL, :] = jnp.zeros((CONV_TAIL, cd), F32)

    x = x_ref[0]
    h = _rms_rows(x, mw_ref[...]).astype(BF16)
    proj_s[...] = _dot(h, win_ref[...])

    cbuf[CONV_TAIL:CONV_TAIL + tc, :] = proj_s[:, d_inner:d_inner + cd]
    conv = cb_ref[...]
    for j in range(CONV_W):
        off = CONV_TAIL - (CONV_W - 1) + j
        conv = conv + cw_ref[j:j + 1, :] * cbuf[off:off + tc, :]
    cbuf[0:CONV_TAIL, :] = cbuf[tc:tc + CONV_TAIL, :]
    proj_s[:, d_inner:d_inner + cd] = conv * _sigmoid(conv)

    dt_pre = proj_s[:, d_inner + cd:d_inner + cd + LANES] + dtb_ref[...]
    dt = jnp.maximum(dt_pre, 0.0) + jnp.log1p(jnp.exp(-jnp.abs(dt_pre)))
    da = dt * (-jnp.exp(alog_ref[...]))

    tri = _tri_ones(q)
    rr = lax.broadcasted_iota(jnp.int32, (q, q), 0)
    cc = lax.broadcasted_iota(jnp.int32, (q, q), 1)
    causal = rr >= cc
    first = lax.broadcasted_iota(jnp.int32, (1, LANES), 1) < SSD_HEADDIM

    for sc in range(tc // q):
        lo, hi = sc * q, (sc + 1) * q
        dtc = dt[lo:hi]
        cs = _cumsum_rows(tri, da[lo:hi])
        last = cs[q - 1:q, :]
        wgt = dtc * jnp.exp(last - cs)
        ecs = jnp.exp(cs)
        glast = jnp.exp(last)
        cs_t = cs.T
        dt_t = dtc.T
        for g in range(SSD_GROUPS):
            bm = proj_s[lo:hi, 2 * d_inner + g * SSD_STATE:2 * d_inner + (g + 1) * SSD_STATE]
            cm = proj_s[lo:hi, 2 * d_inner + gs + g * SSD_STATE:2 * d_inner + gs + (g + 1) * SSD_STATE]
            bmb = bm.astype(BF16)
            cmb = cm.astype(BF16)
            cb = _dot_nt(cmb, bmb)
            for pr in range(pairs_per_group):
                pi = g * pairs_per_group + pr
                h1, h2 = 2 * pi, 2 * pi + 1
                xp = proj_s[lo:hi, d_inner + pi * p2:d_inner + (pi + 1) * p2]
                m1 = cb * jnp.exp(jnp.where(causal, cs[:, h1:h1 + 1] - cs_t[h1:h1 + 1, :], -jnp.inf)) * dt_t[h1:h1 + 1, :]
                m2 = cb * jnp.exp(jnp.where(causal, cs[:, h2:h2 + 1] - cs_t[h2:h2 + 1, :], -jnp.inf)) * dt_t[h2:h2 + 1, :]
                x1 = jnp.where(first, xp, 0.0).astype(BF16)
                x2 = jnp.where(first, 0.0, xp).astype(BF16)
                y = _dot(m1.astype(BF16), x1) + _dot(m2.astype(BF16), x2)
                st = st_s[pi]
                y = y + jnp.where(first, ecs[:, h1:h1 + 1], ecs[:, h2:h2 + 1]) * _dot(cmb, st.astype(BF16))
                wx = (jnp.where(first, wgt[:, h1:h1 + 1], wgt[:, h2:h2 + 1]) * xp).astype(BF16)
                st_s[pi] = st * jnp.where(first, glast[:, h1:h1 + 1], glast[:, h2:h2 + 1]) + _dot_tn(bmb, wx)
                y_s[lo:hi, pi * p2:(pi + 1) * p2] = y + dsk_ref[:, pi * p2:(pi + 1) * p2] * xp

    z = proj_s[:, 0:d_inner]
    y = y_s[...] * (z * _sigmoid(z))
    gw = d_inner // SSD_GROUPS
    parts = []
    for g in range(SSD_GROUPS):
        yg = y[:, g * gw:(g + 1) * gw]
        parts.append(yg * lax.rsqrt(jnp.mean(yg * yg, axis=-1, keepdims=True) + EPS))
    yn = jnp.concatenate(parts, axis=1) * nw_ref[...]
    o_ref[0] = x + _dot(yn.astype(BF16), wout_ref[...])


def _ssd_layer(x, mix_w, w_in, conv_w, conv_b, dt_bias, a_log, d_skip, norm_w, w_out):
    bsz, seq, d = x.shape
    d_inner = w_out.shape[0]
    nheads = dt_bias.shape[0]
    assert d_inner == nheads * SSD_HEADDIM and nheads <= LANES
    cd = d_inner + 2 * SSD_GROUPS * SSD_STATE
    assert conv_w.shape == (CONV_W, cd) and w_in.shape[1] == d_inner + cd + nheads
    tc = SSD_TOKENS
    assert seq % tc == 0 and tc % SSD_CHUNK == 0
    pad = LANES - nheads
    w_all = jnp.pad(w_in, ((0, 0), (0, pad))).astype(BF16)
    n_all = w_all.shape[1]
    const = lambda b, j: (0, 0)
    return pl.pallas_call(
        functools.partial(_ssd_kernel, d_inner=d_inner),
        grid=(bsz, seq // tc),
        in_specs=[
            pl.BlockSpec((1, tc, d), lambda b, j: (b, j, 0)),
            pl.BlockSpec((1, d), const),
            pl.BlockSpec(w_all.shape, const, pipeline_mode=pl.Buffered(1)),
            pl.BlockSpec((CONV_W, cd), const),
            pl.BlockSpec((1, cd), const),
            pl.BlockSpec((1, LANES), const),
            pl.BlockSpec((1, LANES), const),
            pl.BlockSpec((1, d_inner), const),
            pl.BlockSpec((1, d_inner), const),
            pl.BlockSpec(w_out.shape, const, pipeline_mode=pl.Buffered(1)),
        ],
        out_specs=pl.BlockSpec((1, tc, d), lambda b, j: (b, j, 0)),
        out_shape=jax.ShapeDtypeStruct(x.shape, F32),
        scratch_shapes=[
            pltpu.VMEM((tc, n_all), F32),
            pltpu.VMEM((tc + CONV_TAIL, cd), F32),
            pltpu.VMEM((tc, d_inner), F32),
            pltpu.VMEM((nheads // 2, SSD_STATE, LANES), F32),
        ],
        compiler_params=pltpu.CompilerParams(
            dimension_semantics=("arbitrary", "arbitrary"), vmem_limit_bytes=VMEM_LIMIT),
        name="ssd_mixer",
    )(x, mix_w.reshape(1, d), w_all, conv_w, conv_b.reshape(1, cd),
      jnp.pad(dt_bias, (0, pad)).reshape(1, LANES), jnp.pad(a_log, (0, pad)).reshape(1, LANES),
      jnp.repeat(d_skip, SSD_HEADDIM).reshape(1, d_inner), norm_w.reshape(1, d_inner), w_out.astype(BF16))


def _router_kernel(x_ref, nw_ref, rw_ref, idx_ref, gate_ref):
    h = _rms_rows(x_ref[...], nw_ref[...])
    logits = jnp.dot(h, rw_ref[...], precision=lax.Precision.HIGHEST, preferred_element_type=F32)
    lane = lax.broadcasted_iota(jnp.int32, logits.shape, 1)
    valid = lane < MOE_EXPERTS
    lg = jnp.where(valid, logits, -jnp.inf)
    e = jnp.exp(lg - jnp.max(lg, axis=-1, keepdims=True))
    p = jnp.where(valid, e / jnp.sum(e, axis=-1, keepdims=True), -1.0)
    m1 = jnp.max(p, axis=-1, keepdims=True)
    i1 = jnp.min(jnp.where(p == m1, lane, LANES), axis=-1, keepdims=True)
    p2 = jnp.where(lane == i1, -1.0, p)
    m2 = jnp.max(p2, axis=-1, keepdims=True)
    i2 = jnp.min(jnp.where(p2 == m2, lane, LANES), axis=-1, keepdims=True)
    s = m1 + m2
    idx_ref[...] = jnp.where(lane == 0, i1, jnp.where(lane == 1, i2, 0))
    gate_ref[...] = jnp.where(lane == 0, m1 / s, jnp.where(lane == 1, m2 / s, 0.0))


def _router(x2, norm_w, router_w):
    t, d = x2.shape
    ne = router_w.shape[1]
    assert ne == MOE_EXPERTS and t % ROUTER_TOKENS == 0
    rw = jnp.pad(router_w, ((0, 0), (0, LANES - ne)))
    return pl.pallas_call(
        _router_kernel,
        grid=(t // ROUTER_TOKENS,),
        in_specs=[
            pl.BlockSpec((ROUTER_TOKENS, d), lambda i: (i, 0)),
            pl.BlockSpec((1, d), lambda i: (0, 0)),
            pl.BlockSpec((d, LANES), lambda i: (0, 0)),
        ],
        out_specs=[pl.BlockSpec((ROUTER_TOKENS, LANES), lambda i: (i, 0))] * 2,
        out_shape=[jax.ShapeDtypeStruct((t, LANES), jnp.int32), jax.ShapeDtypeStruct((t, LANES), F32)],
        compiler_params=pltpu.CompilerParams(dimension_semantics=("arbitrary",)),
        name="moe_router",
    )(x2, norm_w.reshape(1, d), rw)


def _row_gather_start(src_hbm, dst, row_of, n, sem):
    def issue(r, carry):
        pltpu.make_async_copy(src_hbm.at[pl.ds(row_of(r), 1), :], dst.at[pl.ds(r, 1), :], sem).start()
        return carry
    lax.fori_loop(0, n, issue, 0, unroll=8)


def _row_gather_wait(src_hbm, dst, n, sem):
    pltpu.make_async_copy(src_hbm.at[pl.ds(0, n), :], dst, sem).wait()


def _dispatch_kernel(pad_ref, pos_ref, x_hbm, zero_hbm, xs_hbm, sem, pad_sem):
    i = pl.program_id(0)
    last = pl.num_programs(0) - 1
    tb = pos_ref.shape[2] // 2
    base = i * tb

    def issue(r, carry):
        src = x_hbm.at[pl.ds(base + r, 1), :]
        pltpu.make_async_copy(src, xs_hbm.at[pl.ds(pos_ref[0, 0, 2 * r], 1), :], sem).start()
        pltpu.make_async_copy(src, xs_hbm.at[pl.ds(pos_ref[0, 0, 2 * r + 1], 1), :], sem).start()
        return carry

    lax.fori_loop(0, tb, issue, 0, unroll=4)

    def wait_step():
        pltpu.make_async_copy(x_hbm.at[pl.ds(0, 2 * tb), :], xs_hbm.at[pl.ds(0, 2 * tb), :], sem).wait()

    @pl.when(i > 0)
    def _():
        wait_step()

    @pl.when(i == last)
    def _():
        wait_step()
        for e in range(MOE_EXPERTS):
            first = pad_ref[e]
            count = pad_ref[MOE_EXPERTS + e]

            def zero_issue(r, carry, first=first):
                pltpu.make_async_copy(zero_hbm.at[pl.ds(0, 1), :], xs_hbm.at[pl.ds(first + r, 1), :], pad_sem).start()
                return carry

            def zero_wait(r, carry, first=first):
                pltpu.make_async_copy(zero_hbm.at[pl.ds(0, 1), :], xs_hbm.at[pl.ds(first, 1), :], pad_sem).wait()
                return carry

            lax.fori_loop(0, count, zero_issue, 0)
            lax.fori_loop(0, count, zero_wait, 0)

        tm = zero_hbm.shape[0]
        first_tile = pad_ref[2 * MOE_EXPERTS]

        def tile_copy(j):
            return pltpu.make_async_copy(zero_hbm, xs_hbm.at[pl.ds((first_tile + j) * tm, tm), :], pad_sem)

        def tile_issue(j, carry):
            tile_copy(j).start()
            return carry

        def tile_wait(j, carry):
            tile_copy(j).wait()
            return carry

        n_unused = xs_hbm.shape[0] // tm - first_tile
        lax.fori_loop(0, n_unused, tile_issue, 0)
        lax.fori_loop(0, n_unused, tile_wait, 0)


def _dispatch(x2, pos, pad_info, n_rows):
    t, d = x2.shape
    tb = COMBINE_TOKENS
    assert t % tb == 0
    grid_spec = pltpu.PrefetchScalarGridSpec(
        num_scalar_prefetch=1,
        grid=(t // tb,),
        in_specs=[
            pl.BlockSpec((1, 1, 2 * tb), lambda i, pad: (i, 0, 0), memory_space=pltpu.SMEM),
            pl.BlockSpec(memory_space=pl.ANY),
            pl.BlockSpec(memory_space=pl.ANY),
        ],
        out_specs=pl.BlockSpec(memory_space=pl.ANY),
        scratch_shapes=[pltpu.SemaphoreType.DMA(()), pltpu.SemaphoreType.DMA(())],
    )
    return pl.pallas_call(
        _dispatch_kernel,
        grid_spec=grid_spec,
        out_shape=jax.ShapeDtypeStruct((n_rows, d), F32),
        compiler_params=pltpu.CompilerParams(dimension_semantics=("arbitrary",)),
        name="moe_dispatch",
    )(pad_info, pos.reshape(t // tb, 1, 2 * tb), x2, jnp.zeros((MOE_TILE, d), F32))


def _experts_kernel(te_ref, tv_ref, ts_ref, xs_ref, nw_ref, wg_ref, wu_ref, wd_ref, y_ref):
    i = pl.program_id(0)

    @pl.when(tv_ref[i] > 0)
    def _():
        h = _rms_rows(xs_ref[...], nw_ref[...]).astype(BF16)
        y_ref[...] = _swiglu_rows(h, wg_ref.at[0], wu_ref.at[0], wd_ref.at[0], None)

    @pl.when(tv_ref[i] == 0)
    def _():
        y_ref[...] = jnp.zeros_like(y_ref)


def _experts(xs, norm_w, tile_expert, tile_valid, tile_src, w_gate, w_up, w_down):
    n_rows, d = xs.shape
    ne, _, dff = w_gate.shape
    tm = MOE_TILE
    nt = n_rows // tm
    wmap = lambda i, te, tv, ts: (te[i], 0, 0)
    grid_spec = pltpu.PrefetchScalarGridSpec(
        num_scalar_prefetch=3,
        grid=(nt,),
        in_specs=[
            pl.BlockSpec((tm, d), lambda i, te, tv, ts: (ts[i], 0)),
            pl.BlockSpec((1, d), lambda i, te, tv, ts: (0, 0)),
            pl.BlockSpec((1, d, dff), wmap),
            pl.BlockSpec((1, d, dff), wmap),
            pl.BlockSpec((1, dff, d), wmap),
        ],
        out_specs=pl.BlockSpec((tm, d), lambda i, te, tv, ts: (i, 0)),
    )
    return pl.pallas_call(
        _experts_kernel,
        grid_spec=grid_spec,
        out_shape=jax.ShapeDtypeStruct((n_rows, d), F32),
        compiler_params=pltpu.CompilerParams(
            dimension_semantics=("arbitrary",), vmem_limit_bytes=VMEM_LIMIT),
        name="moe_experts",
    )(tile_expert, tile_valid, tile_src, xs, norm_w.reshape(1, d),
      w_gate.astype(BF16), w_up.astype(BF16), w_down.astype(BF16))


def _combine_kernel(pos_ref, posn_ref, x_ref, gate_ref, fw_ref, y_hbm, o_ref, ybuf, sem):
    i = pl.program_id(0)
    tb = x_ref.shape[0]
    slot = lax.rem(i, 2)

    def start(p_ref, s):
        for j in range(2):
            _row_gather_start(y_hbm, ybuf.at[s, j], lambda r, j=j: p_ref[0, 0, 2 * r + j], tb, sem.at[s, j])

    @pl.when(i == 0)
    def _():
        start(pos_ref, 0)

    @pl.when(i + 1 < pl.num_programs(0))
    def _():
        start(posn_ref, 1 - slot)

    for j in range(2):
        _row_gather_wait(y_hbm, ybuf.at[slot, j], tb, sem.at[slot, j])
    g = gate_ref[...]
    o = x_ref[...] + g[:, 0:1] * ybuf[slot, 0] + g[:, 1:2] * ybuf[slot, 1]
    o_ref[...] = _rms_rows(o, fw_ref[...])


def _combine(x2, final_w, pos, gate_pad, y_rows):
    t, d = x2.shape
    tb = COMBINE_TOKENS
    nb = t // tb
    pos3 = pos.reshape(nb, 1, 2 * tb)
    return pl.pallas_call(
        _combine_kernel,
        grid=(nb,),
        in_specs=[
            pl.BlockSpec((1, 1, 2 * tb), lambda i: (i, 0, 0), memory_space=pltpu.SMEM),
            pl.BlockSpec((1, 1, 2 * tb), lambda i: (jnp.minimum(i + 1, nb - 1), 0, 0), memory_space=pltpu.SMEM),
            pl.BlockSpec((tb, d), lambda i: (i, 0)),
            pl.BlockSpec((tb, LANES), lambda i: (i, 0)),
            pl.BlockSpec((1, d), lambda i: (0, 0)),
            pl.BlockSpec(memory_space=pl.ANY),
        ],
        out_specs=pl.BlockSpec((tb, d), lambda i: (i, 0)),
        out_shape=jax.ShapeDtypeStruct(x2.shape, F32),
        scratch_shapes=[pltpu.VMEM((2, 2, tb, d), F32), pltpu.SemaphoreType.DMA((2, 2))],
        compiler_params=pltpu.CompilerParams(dimension_semantics=("arbitrary",)),
        name="moe_combine",
    )(pos3, pos3, x2, gate_pad, final_w.reshape(1, d), y_rows)


def _moe_layer_and_final_norm(x2, norm_w, final_w, router_w, w_gate, w_up, w_down):
    t, d = x2.shape
    ne = MOE_EXPERTS
    tm = MOE_TILE
    idx_pad, gate_pad = _router(x2, norm_w, router_w)
    flat_e = idx_pad[:, :2].reshape(-1)
    onehot = (flat_e[:, None] == jnp.arange(ne, dtype=jnp.int32)[None, :]).astype(jnp.int32)
    csum = jnp.cumsum(onehot, axis=0)
    rank = jnp.sum(csum * onehot, axis=1) - 1
    counts = csum[-1]
    padded = ((counts + tm - 1) // tm) * tm
    ends = jnp.cumsum(padded)
    starts = ends - padded
    pos = (jnp.sum(starts[None, :] * onehot, axis=1) + rank).astype(jnp.int32)
    nt = (2 * t) // tm + ne
    tile_ids = jnp.arange(nt, dtype=jnp.int32)
    tile_valid = (tile_ids * tm < ends[-1]).astype(jnp.int32)
    tile_expert = jnp.minimum(jnp.sum((tile_ids[:, None] * tm >= ends[None, :]).astype(jnp.int32), axis=1), ne - 1)
    tile_src = tile_ids * tile_valid
    pad_info = jnp.concatenate([starts + counts, padded - counts, ends[-1:] // tm]).astype(jnp.int32)
    xs = _dispatch(x2, pos, pad_info, nt * tm)
    y_rows = _experts(xs, norm_w, tile_expert.astype(jnp.int32), tile_valid, tile_src, w_gate, w_up, w_down)
    return _combine(x2, final_w, pos, gate_pad, y_rows)


def kernel(x, mix_norm_w, ffn_norm_w, final_norm_w, hg_w_in, hg_lb_logits, hg_norm_w, hg_w_out, ssd_w_in, ssd_conv_w, ssd_conv_b, ssd_dt_bias, ssd_a_log, ssd_d, ssd_norm_w, ssd_w_out, ffn_w_gate, ffn_w_up, ffn_w_down, moe_router, moe_w_gate, moe_w_up, moe_w_down):
    bsz, seq, d = x.shape
    assert mix_norm_w.shape[0] == 2 and hg_w_in.shape[0] == 1 and ssd_w_in.shape[0] == 1
    x = _hgrn_layer(x, mix_norm_w[0], hg_w_in[0], hg_lb_logits, hg_norm_w[0], hg_w_out[0])
    x2 = _ffn_layer(x.reshape(bsz * seq, d), ffn_norm_w[0], ffn_w_gate[0], ffn_w_up[0], ffn_w_down[0])
    x = _ssd_layer(x2.reshape(bsz, seq, d), mix_norm_w[1], ssd_w_in[0], ssd_conv_w[0], ssd_conv_b[0],
                   ssd_dt_bias[0], ssd_a_log[0], ssd_d[0], ssd_norm_w[0], ssd_w_out[0])
    out = _moe_layer_and_final_norm(x.reshape(bsz * seq, d), ffn_norm_w[1], final_norm_w, moe_router[0],
                                    moe_w_gate[0], moe_w_up[0], moe_w_down[0])
    return out.reshape(bsz, seq, d)
```

```python
import functools

import jax
import jax.numpy as jnp
from jax import lax
from jax.experimental import pallas as pl
from jax.experimental.pallas import tpu as pltpu

F32 = jnp.float32
BF16 = jnp.bfloat16
EPS = 1e-6
LANES = 128
VMEM_LIMIT = 60000 * 1024

HG_CHUNK = 256
HG_LEVELS = (128, 64, 32, 16, 8, 4, 2, 1)
HG_TOKENS = 256
SSD_CHUNK = 128
SSD_TOKENS = 256
SSD_HEADDIM = 64
SSD_STATE = 128
SSD_GROUPS = 4
CONV_W = 4
CONV_TAIL = 8
FFN_TOKENS = 512
FFN_CHUNK = 1408
MOE_EXPERTS = 8
MOE_TILE = 512
ROUTER_TOKENS = 512
COMBINE_TOKENS = 256


def _dot(a, b):
    return jnp.dot(a, b, preferred_element_type=F32)


def _dot_nt(a, b):
    return lax.dot_general(a, b, (((1,), (1,)), ((), ())), preferred_element_type=F32)


def _dot_tn(a, b):
    return lax.dot_general(a, b, (((0,), (0,)), ((), ())), preferred_element_type=F32)


def _sigmoid(x):
    return 0.5 * jnp.tanh(0.5 * x) + 0.5


def _rms_rows(x, w):
    return x * lax.rsqrt(jnp.mean(x * x, axis=-1, keepdims=True) + EPS) * w


def _tri_ones(n):
    r = lax.broadcasted_iota(jnp.int32, (n, n), 0)
    c = lax.broadcasted_iota(jnp.int32, (n, n), 1)
    return jnp.where(r >= c, 1.0, 0.0).astype(BF16)


def _cumsum_rows(tri, x):
    hi = x.astype(BF16)
    r1 = x - hi.astype(F32)
    mid = r1.astype(BF16)
    lo = (r1 - mid.astype(F32)).astype(BF16)
    return _dot(tri, hi) + _dot(tri, mid) + _dot(tri, lo)


def _hgrn_kernel(x_ref, mw_ref, win_ref, lbl_ref, nw_ref, wout_ref, o_ref, proj_s, st_s, *, nh, dk, dv):
    tc = x_ref.shape[1]
    qk = nh * dk
    vd = nh * dv

    @pl.when(pl.program_id(1) == 0)
    def _():
        st_s[...] = jnp.zeros_like(st_s)

    x = x_ref[0]
    h = _rms_rows(x, mw_ref[...]).astype(BF16)
    for r0 in range(0, tc, HG_CHUNK):
        proj_s[r0:r0 + HG_CHUNK, :] = _dot(h[r0:r0 + HG_CHUNK], win_ref[...])

    lbl = lbl_ref[...]
    lmax = jnp.max(lbl, axis=0, keepdims=True)
    le = jnp.exp(lbl - lmax)
    lb = le[0:1, :] / jnp.sum(le, axis=0, keepdims=True)

    tri = _tri_ones(HG_CHUNK)
    ti = lax.broadcasted_iota(jnp.int32, (HG_CHUNK, HG_CHUNK), 0)
    si = lax.broadcasted_iota(jnp.int32, (HG_CHUNK, HG_CHUNK), 1)
    level = jnp.where(si > ti, -2, 31 - lax.clz(ti ^ si))
    level_ids = [hs.bit_length() - 1 for hs in HG_LEVELS] + [-1]
    sub8 = lax.broadcasted_iota(jnp.int32, (HG_CHUNK // 8, 8, qk), 1)

    def boundary_rows(b, hs):
        if hs >= 8:
            parts = [jnp.broadcast_to(b[lo + hs - 1:lo + hs, :], (2 * hs, qk)) for lo in range(0, HG_CHUNK, 2 * hs)]
            return parts[0] if len(parts) == 1 else jnp.concatenate(parts, axis=0)
        b3 = b.reshape(HG_CHUNK // 8, 8, qk)
        ref = jnp.broadcast_to(b3[:, hs - 1:hs, :], b3.shape)
        for lo in range(2 * hs, 8, 2 * hs):
            ref = jnp.where(sub8 >= lo, jnp.broadcast_to(b3[:, lo + hs - 1:lo + hs, :], b3.shape), ref)
        return ref.reshape(HG_CHUNK, qk)

    def chunk_body(c):
        r0 = c * HG_CHUNK
        q = proj_s[pl.ds(r0, HG_CHUNK), 0:qk]
        fp = proj_s[pl.ds(r0, HG_CHUNK), qk:2 * qk]
        v = proj_s[pl.ds(r0, HG_CHUNK), 2 * qk:2 * qk + vd]
        f = lb + (1.0 - lb) * _sigmoid(fp)
        k = 1.0 - f
        b = _cumsum_rows(tri, jnp.log2(f))
        b_last = b[HG_CHUNK - 1:HG_CHUNK, :]
        qg = (q * jnp.exp2(b)).astype(BF16)
        kdec = (k * jnp.exp2(b_last - b)).astype(BF16)
        g_last = jnp.exp2(b_last)
        vb = v.astype(BF16)

        qds = []
        kds = []
        for hs in HG_LEVELS:
            z = jnp.exp2(-jnp.abs(b - boundary_rows(b, hs)))
            qds.append((q * z).astype(BF16))
            kds.append((k * z).astype(BF16))
        qds.append(q.astype(BF16))
        kds.append(k.astype(BF16))

        o_heads = []
        for hh in range(nh):
            ks = slice(hh * dk, (hh + 1) * dk)
            attn = jnp.zeros((HG_CHUNK, HG_CHUNK), F32)
            for qd, kd, lid in zip(qds, kds, level_ids):
                attn = jnp.where(level == lid, _dot_nt(qd[:, ks], kd[:, ks]), attn)
            st = st_s[hh]
            o_inter = _dot_nt(qg[:, ks], st.astype(BF16))
            oh = o_inter + _dot(attn.astype(BF16), vb[:, hh * dv:(hh + 1) * dv])
            o_heads.append(oh * lax.rsqrt(jnp.mean(oh * oh, axis=-1, keepdims=True) + EPS))
            st_s[hh] = st * g_last[:, ks] + _dot_tn(vb[:, hh * dv:(hh + 1) * dv], kdec[:, ks])
        g = proj_s[pl.ds(r0, HG_CHUNK), 2 * qk + vd:2 * qk + 2 * vd]
        on = jnp.concatenate(o_heads, axis=1) * nw_ref[...] * _sigmoid(g)
        o_ref[0, pl.ds(r0, HG_CHUNK), :] = x[r0:r0 + HG_CHUNK] + _dot(on.astype(BF16), wout_ref[...])

    for c in range(tc // HG_CHUNK):
        chunk_body(c)


def _hgrn_layer(x, mix_w, w_in, lb_logits, norm_w, w_out):
    bsz, seq, d = x.shape
    dv = norm_w.shape[0]
    nh = w_out.shape[0] // dv
    dk = (w_in.shape[1] - 2 * nh * dv) // (2 * nh)
    assert lb_logits.shape[0] == 2 and dk == LANES and dv == LANES
    tc = HG_TOKENS
    assert seq % tc == 0 and tc % HG_CHUNK == 0
    const = lambda b, j: (0, 0)
    return pl.pallas_call(
        functools.partial(_hgrn_kernel, nh=nh, dk=dk, dv=dv),
        grid=(bsz, seq // tc),
        in_specs=[
            pl.BlockSpec((1, tc, d), lambda b, j: (b, j, 0)),
            pl.BlockSpec((1, d), const),
            pl.BlockSpec(w_in.shape, const, pipeline_mode=pl.Buffered(1)),
            pl.BlockSpec(lb_logits.shape, const),
            pl.BlockSpec((1, nh * dv), const),
            pl.BlockSpec(w_out.shape, const, pipeline_mode=pl.Buffered(1)),
        ],
        out_specs=pl.BlockSpec((1, tc, d), lambda b, j: (b, j, 0)),
        out_shape=jax.ShapeDtypeStruct(x.shape, F32),
        scratch_shapes=[
            pltpu.VMEM((tc, w_in.shape[1]), F32),
            pltpu.VMEM((nh, dv, dk), F32),
        ],
        compiler_params=pltpu.CompilerParams(
            dimension_semantics=("arbitrary", "arbitrary"), vmem_limit_bytes=VMEM_LIMIT),
        name="hgrn_mixer",
    )(x, mix_w.reshape(1, d), w_in.astype(BF16), lb_logits, jnp.tile(norm_w, nh).reshape(1, nh * dv),
      w_out.astype(BF16))


def _swiglu_rows(h, wg_ref, wu_ref, wd_ref, acc):
    dff = wg_ref.shape[-1]
    for c0 in range(0, dff, FFN_CHUNK):
        g = _dot(h, wg_ref[:, c0:c0 + FFN_CHUNK])
        u = _dot(h, wu_ref[:, c0:c0 + FFN_CHUNK])
        a = (g * _sigmoid(g) * u).astype(BF16)
        y = _dot(a, wd_ref[c0:c0 + FFN_CHUNK, :])
        acc = y if acc is None else acc + y
    return acc


def _ffn_kernel(x_ref, nw_ref, wg_ref, wu_ref, wd_ref, o_ref):
    x = x_ref[...]
    h = _rms_rows(x, nw_ref[...]).astype(BF16)
    o_ref[...] = _swiglu_rows(h, wg_ref, wu_ref, wd_ref, x)


def _ffn_layer(x2, norm_w, w_gate, w_up, w_down):
    t, d = x2.shape
    dff = w_gate.shape[1]
    assert t % FFN_TOKENS == 0 and dff % FFN_CHUNK == 0
    const = lambda i: (0, 0)
    return pl.pallas_call(
        _ffn_kernel,
        grid=(t // FFN_TOKENS,),
        in_specs=[
            pl.BlockSpec((FFN_TOKENS, d), lambda i: (i, 0)),
            pl.BlockSpec((1, d), const),
            pl.BlockSpec((d, dff), const, pipeline_mode=pl.Buffered(1)),
            pl.BlockSpec((d, dff), const, pipeline_mode=pl.Buffered(1)),
            pl.BlockSpec((dff, d), const, pipeline_mode=pl.Buffered(1)),
        ],
        out_specs=pl.BlockSpec((FFN_TOKENS, d), lambda i: (i, 0)),
        out_shape=jax.ShapeDtypeStruct(x2.shape, F32),
        compiler_params=pltpu.CompilerParams(
            dimension_semantics=("arbitrary",), vmem_limit_bytes=VMEM_LIMIT),
        name="dense_ffn",
    )(x2, norm_w.reshape(1, d), w_gate.astype(BF16), w_up.astype(BF16), w_down.astype(BF16))


def _ssd_kernel(x_ref, mw_ref, win_ref, cw_ref, cb_ref, dtb_ref, alog_ref, dsk_ref, nw_ref, wout_ref, o_ref,
                proj_s, cbuf, y_s, st_s, *, d_inner):
    tc = x_ref.shape[1]
    p2 = 2 * SSD_HEADDIM
    assert p2 == LANES and SSD_STATE == LANES
    gs = SSD_GROUPS * SSD_STATE
    cd = d_inner + 2 * gs
    pairs_per_group = d_inner // (SSD_GROUPS * p2)
    q = SSD_CHUNK
    assert q == LANES

    @pl.when(pl.program_id(1) == 0)
    def _():
        st_s[...] = jnp.zeros_like(st_s)
        cbuf[:, 0:CONV_TAIL, :] = jnp.zeros((cd // LANES, CONV_TAIL, LANES), F32)

    x = x_ref[0]
    h = _rms_rows(x, mw_ref[...]).astype(BF16)
    proj_s[...] = _dot(h, win_ref[...])

    for j in range(cd // LANES):
        c0 = d_inner + j * LANES
        cbuf[j, CONV_TAIL:CONV_TAIL + tc, :] = proj_s[:, c0:c0 + LANES]
        conv = cb_ref[:, j * LANES:(j + 1) * LANES]
        for k in range(CONV_W):
            off = CONV_TAIL - (CONV_W - 1) + k
            conv = conv + cw_ref[k:k + 1, j * LANES:(j + 1) * LANES] * cbuf[j, off:off + tc, :]
        cbuf[j, 0:CONV_TAIL, :] = cbuf[j, tc:tc + CONV_TAIL, :]
        proj_s[:, c0:c0 + LANES] = conv * _sigmoid(conv)

    dt_pre = proj_s[:, d_inner + cd:d_inner + cd + LANES] + dtb_ref[...]
    dt = jnp.maximum(dt_pre, 0.0) + jnp.log1p(jnp.exp(-jnp.abs(dt_pre)))
    da = dt * (-jnp.exp(alog_ref[...]))

    tri = _tri_ones(q)
    rr = lax.broadcasted_iota(jnp.int32, (q, q), 0)
    cc = lax.broadcasted_iota(jnp.int32, (q, q), 1)
    causal = rr >= cc
    first = lax.broadcasted_iota(jnp.int32, (1, LANES), 1) < SSD_HEADDIM

    def lane_col(a, hd):
        return jnp.broadcast_to(a[:, hd:hd + 1], (q, LANES))

    for sc in range(tc // q):
        lo, hi = sc * q, (sc + 1) * q
        dtc = dt[lo:hi]
        cs = _cumsum_rows(tri, da[lo:hi])
        last = cs[q - 1:q, :]
        cs_t = cs.T
        for g in range(SSD_GROUPS):
            bm = proj_s[lo:hi, 2 * d_inner + g * SSD_STATE:2 * d_inner + (g + 1) * SSD_STATE]
            cm = proj_s[lo:hi, 2 * d_inner + gs + g * SSD_STATE:2 * d_inner + gs + (g + 1) * SSD_STATE]
            bmb = bm.astype(BF16)
            cmb = cm.astype(BF16)
            cb = _dot_nt(cmb, bmb)
            for pr in range(pairs_per_group):
                pi = g * pairs_per_group + pr
                h1, h2 = 2 * pi, 2 * pi + 1
                xp = proj_s[lo:hi, d_inner + pi * p2:d_inner + (pi + 1) * p2]
                c1 = lane_col(cs, h1)
                c2 = lane_col(cs, h2)
                cs_p = jnp.where(first, c1, c2)
                last_p = jnp.where(first, last[:, h1:h1 + 1], last[:, h2:h2 + 1])
                xdt = jnp.where(first, lane_col(dtc, h1), lane_col(dtc, h2)) * xp
                m1 = cb * jnp.exp(jnp.where(causal, c1 - cs_t[h1:h1 + 1, :], -jnp.inf))
                m2 = cb * jnp.exp(jnp.where(causal, c2 - cs_t[h2:h2 + 1, :], -jnp.inf))
                x1 = jnp.where(first, xdt, 0.0).astype(BF16)
                x2 = jnp.where(first, 0.0, xdt).astype(BF16)
                y = _dot(m1.astype(BF16), x1) + _dot(m2.astype(BF16), x2)
                st = st_s[pi]
                y = y + jnp.exp(cs_p) * _dot(cmb, st.astype(BF16))
                wx = (jnp.exp(last_p - cs_p) * xdt).astype(BF16)
                st_s[pi] = st * jnp.exp(last_p) + _dot_tn(bmb, wx)
                y_s[lo:hi, pi * p2:(pi + 1) * p2] = y + dsk_ref[:, pi * p2:(pi + 1) * p2] * xp

    z = proj_s[:, 0:d_inner]
    y = y_s[...] * (z * _sigmoid(z))
    gw = d_inner // SSD_GROUPS
    parts = []
    for g in range(SSD_GROUPS):
        yg = y[:, g * gw:(g + 1) * gw]
        parts.append(yg * lax.rsqrt(jnp.mean(yg * yg, axis=-1, keepdims=True) + EPS))
    yn = jnp.concatenate(parts, axis=1) * nw_ref[...]
    o_ref[0] = x + _dot(yn.astype(BF16), wout_ref[...])


def _ssd_layer(x, mix_w, w_in, conv_w, conv_b, dt_bias, a_log, d_skip, norm_w, w_out):
    bsz, seq, d = x.shape
    d_inner = w_out.shape[0]
    nheads = dt_bias.shape[0]
    assert d_inner == nheads * SSD_HEADDIM and nheads <= LANES
    cd = d_inner + 2 * SSD_GROUPS * SSD_STATE
    assert conv_w.shape == (CONV_W, cd) and w_in.shape[1] == d_inner + cd + nheads
    tc = SSD_TOKENS
    assert seq % tc == 0 and tc % SSD_CHUNK == 0
    pad = LANES - nheads
    w_all = jnp.pad(w_in, ((0, 0), (0, pad))).astype(BF16)
    n_all = w_all.shape[1]
    const = lambda b, j: (0, 0)
    return pl.pallas_call(
        functools.partial(_ssd_kernel, d_inner=d_inner),
        grid=(bsz, seq // tc),
        in_specs=[
            pl.BlockSpec((1, tc, d), lambda b, j: (b, j, 0)),
            pl.BlockSpec((1, d), const),
            pl.BlockSpec(w_all.shape, const, pipeline_mode=pl.Buffered(1)),
            pl.BlockSpec((CONV_W, cd), const),
            pl.BlockSpec((1, cd), const),
            pl.BlockSpec((1, LANES), const),
            pl.BlockSpec((1, LANES), const),
            pl.BlockSpec((1, d_inner), const),
            pl.BlockSpec((1, d_inner), const),
            pl.BlockSpec(w_out.shape, const, pipeline_mode=pl.Buffered(1)),
        ],
        out_specs=pl.BlockSpec((1, tc, d), lambda b, j: (b, j, 0)),
        out_shape=jax.ShapeDtypeStruct(x.shape, F32),
        scratch_shapes=[
            pltpu.VMEM((tc, n_all), F32),
            pltpu.VMEM((cd // LANES, tc + CONV_TAIL, LANES), F32),
            pltpu.VMEM((tc, d_inner), F32),
            pltpu.VMEM((nheads // 2, SSD_STATE, LANES), F32),
        ],
        compiler_params=pltpu.CompilerParams(
            dimension_semantics=("arbitrary", "arbitrary"), vmem_limit_bytes=VMEM_LIMIT),
        name="ssd_mixer",
    )(x, mix_w.reshape(1, d), w_all, conv_w, conv_b.reshape(1, cd),
      jnp.pad(dt_bias, (0, pad)).reshape(1, LANES), jnp.pad(a_log, (0, pad)).reshape(1, LANES),
      jnp.repeat(d_skip, SSD_HEADDIM).reshape(1, d_inner), norm_w.reshape(1, d_inner), w_out.astype(BF16))


def _router_kernel(x_ref, nw_ref, rw_ref, idx_ref, gate_ref):
    h = _rms_rows(x_ref[...], nw_ref[...])
    logits = jnp.dot(h, rw_ref[...], precision=lax.Precision.HIGHEST, preferred_element_type=F32)
    lane = lax.broadcasted_iota(jnp.int32, logits.shape, 1)
    valid = lane < MOE_EXPERTS
    lg = jnp.where(valid, logits, -jnp.inf)
    e = jnp.exp(lg - jnp.max(lg, axis=-1, keepdims=True))
    p = jnp.where(valid, e / jnp.sum(e, axis=-1, keepdims=True), -1.0)
    m1 = jnp.max(p, axis=-1, keepdims=True)
    i1 = jnp.min(jnp.where(p == m1, lane, LANES), axis=-1, keepdims=True)
    p2 = jnp.where(lane == i1, -1.0, p)
    m2 = jnp.max(p2, axis=-1, keepdims=True)
    i2 = jnp.min(jnp.where(p2 == m2, lane, LANES), axis=-1, keepdims=True)
    s = m1 + m2
    idx_ref[...] = jnp.where(lane == 0, i1, jnp.where(lane == 1, i2, 0))
    gate_ref[...] = jnp.where(lane == 0, m1 / s, jnp.where(lane == 1, m2 / s, 0.0))


def _router(x2, norm_w, router_w):
    t, d = x2.shape
    ne = router_w.shape[1]
    assert ne == MOE_EXPERTS and t % ROUTER_TOKENS == 0
    rw = jnp.pad(router_w, ((0, 0), (0, LANES - ne)))
    return pl.pallas_call(
        _router_kernel,
        grid=(t // ROUTER_TOKENS,),
        in_specs=[
            pl.BlockSpec((ROUTER_TOKENS, d), lambda i: (i, 0)),
            pl.BlockSpec((1, d), lambda i: (0, 0)),
            pl.BlockSpec((d, LANES), lambda i: (0, 0)),
        ],
        out_specs=[pl.BlockSpec((ROUTER_TOKENS, LANES), lambda i: (i, 0))] * 2,
        out_shape=[jax.ShapeDtypeStruct((t, LANES), jnp.int32), jax.ShapeDtypeStruct((t, LANES), F32)],
        compiler_params=pltpu.CompilerParams(dimension_semantics=("arbitrary",)),
        name="moe_router",
    )(x2, norm_w.reshape(1, d), rw)


def _row_gather_start(src_hbm, dst, row_of, n, sem):
    def issue(r, carry):
        pltpu.make_async_copy(src_hbm.at[pl.ds(row_of(r), 1), :], dst.at[pl.ds(r, 1), :], sem).start()
        return carry
    lax.fori_loop(0, n, issue, 0, unroll=8)


def _row_gather_wait(src_hbm, dst, n, sem):
    pltpu.make_async_copy(src_hbm.at[pl.ds(0, n), :], dst, sem).wait()


def _dispatch_kernel(pad_ref, pos_ref, x_ref, xs_hbm, xbuf, zbuf, sem, pad_sem):
    i = pl.program_id(0)
    last = pl.num_programs(0) - 1
    tb = x_ref.shape[0]
    slot = lax.rem(i, 2)

    xbuf[slot] = x_ref[...]

    def issue(r, carry):
        src = xbuf.at[slot, pl.ds(r, 1), :]
        pltpu.make_async_copy(src, xs_hbm.at[pl.ds(pos_ref[0, 0, 2 * r], 1), :], sem.at[slot]).start()
        pltpu.make_async_copy(src, xs_hbm.at[pl.ds(pos_ref[0, 0, 2 * r + 1], 1), :], sem.at[slot]).start()
        return carry

    lax.fori_loop(0, tb, issue, 0, unroll=4)

    def wait_slot(s):
        for _ in range(2):
            pltpu.make_async_copy(xbuf.at[s], xs_hbm.at[pl.ds(0, tb), :], sem.at[s]).wait()

    @pl.when(i > 0)
    def _():
        wait_slot(1 - slot)

    @pl.when(i == last)
    def _():
        wait_slot(slot)
        zbuf[...] = jnp.zeros_like(zbuf)
        for e in range(MOE_EXPERTS):
            first = pad_ref[e]
            count = pad_ref[MOE_EXPERTS + e]

            def zero_issue(r, carry, first=first):
                pltpu.make_async_copy(zbuf.at[pl.ds(0, 1), :], xs_hbm.at[pl.ds(first + r, 1), :], pad_sem).start()
                return carry

            def zero_wait(r, carry, first=first):
                pltpu.make_async_copy(zbuf.at[pl.ds(0, 1), :], xs_hbm.at[pl.ds(first, 1), :], pad_sem).wait()
                return carry

            lax.fori_loop(0, count, zero_issue, 0)
            lax.fori_loop(0, count, zero_wait, 0)

        tm = zbuf.shape[0]
        first_tile = pad_ref[2 * MOE_EXPERTS]

        def tile_copy(j):
            return pltpu.make_async_copy(zbuf, xs_hbm.at[pl.ds((first_tile + j) * tm, tm), :], pad_sem)

        def tile_issue(j, carry):
            tile_copy(j).start()
            return carry

        def tile_wait(j, carry):
            tile_copy(j).wait()
            return carry

        n_unused = xs_hbm.shape[0] // tm - first_tile
        lax.fori_loop(0, n_unused, tile_issue, 0)
        lax.fori_loop(0, n_unused, tile_wait, 0)


def _dispatch(x2, pos, pad_info, n_rows):
    t, d = x2.shape
    tb = COMBINE_TOKENS
    assert t % tb == 0
    grid_spec = pltpu.PrefetchScalarGridSpec(
        num_scalar_prefetch=1,
        grid=(t // tb,),
        in_specs=[
            pl.BlockSpec((1, 1, 2 * tb), lambda i, pad: (i, 0, 0), memory_space=pltpu.SMEM),
            pl.BlockSpec((tb, d), lambda i, pad: (i, 0)),
        ],
        out_specs=pl.BlockSpec(memory_space=pl.ANY),
        scratch_shapes=[pltpu.VMEM((2, tb, d), F32), pltpu.VMEM((MOE_TILE, d), F32),
                        pltpu.SemaphoreType.DMA((2,)), pltpu.SemaphoreType.DMA(())],
    )
    return pl.pallas_call(
        _dispatch_kernel,
        grid_spec=grid_spec,
        out_shape=jax.ShapeDtypeStruct((n_rows, d), F32),
        compiler_params=pltpu.CompilerParams(dimension_semantics=("arbitrary",)),
        name="moe_dispatch",
    )(pad_info, pos.reshape(t // tb, 1, 2 * tb), x2)


def _experts_kernel(te_ref, tv_ref, ts_ref, xs_ref, nw_ref, wg_ref, wu_ref, wd_ref, y_ref):
    i = pl.program_id(0)

    @pl.when(tv_ref[i] > 0)
    def _():
        h = _rms_rows(xs_ref[...], nw_ref[...]).astype(BF16)
        y_ref[...] = _swiglu_rows(h, wg_ref.at[0], wu_ref.at[0], wd_ref.at[0], None)

    @pl.when(tv_ref[i] == 0)
    def _():
        y_ref[...] = jnp.zeros_like(y_ref)


def _experts(xs, norm_w, tile_expert, tile_valid, tile_src, w_gate, w_up, w_down):
    n_rows, d = xs.shape
    ne, _, dff = w_gate.shape
    tm = MOE_TILE
    nt = n_rows // tm
    wmap = lambda i, te, tv, ts: (te[i], 0, 0)
    grid_spec = pltpu.PrefetchScalarGridSpec(
        num_scalar_prefetch=3,
        grid=(nt,),
        in_specs=[
            pl.BlockSpec((tm, d), lambda i, te, tv, ts: (ts[i], 0)),
            pl.BlockSpec((1, d), lambda i, te, tv, ts: (0, 0)),
            pl.BlockSpec((1, d, dff), wmap),
            pl.BlockSpec((1, d, dff), wmap),
            pl.BlockSpec((1, dff, d), wmap),
        ],
        out_specs=pl.BlockSpec((tm, d), lambda i, te, tv, ts: (i, 0)),
    )
    return pl.pallas_call(
        _experts_kernel,
        grid_spec=grid_spec,
        out_shape=jax.ShapeDtypeStruct((n_rows, d), F32),
        compiler_params=pltpu.CompilerParams(
            dimension_semantics=("arbitrary",), vmem_limit_bytes=VMEM_LIMIT),
        name="moe_experts",
    )(tile_expert, tile_valid, tile_src, xs, norm_w.reshape(1, d),
      w_gate.astype(BF16), w_up.astype(BF16), w_down.astype(BF16))


def _combine_kernel(pos_ref, posn_ref, x_ref, gate_ref, fw_ref, y_hbm, o_ref, ybuf, sem):
    i = pl.program_id(0)
    tb = x_ref.shape[0]
    slot = lax.rem(i, 2)

    def start(p_ref, s):
        for j in range(2):
            _row_gather_start(y_hbm, ybuf.at[s, j], lambda r, j=j: p_ref[0, 0, 2 * r + j], tb, sem.at[s, j])

    @pl.when(i == 0)
    def _():
        start(pos_ref, 0)

    @pl.when(i + 1 < pl.num_programs(0))
    def _():
        start(posn_ref, 1 - slot)

    for j in range(2):
        _row_gather_wait(y_hbm, ybuf.at[slot, j], tb, sem.at[slot, j])
    g = gate_ref[...]
    o = x_ref[...] + g[:, 0:1] * ybuf[slot, 0] + g[:, 1:2] * ybuf[slot, 1]
    o_ref[...] = _rms_rows(o, fw_ref[...])


def _combine(x2, final_w, pos, gate_pad, y_rows):
    t, d = x2.shape
    tb = COMBINE_TOKENS
    nb = t // tb
    pos3 = pos.reshape(nb, 1, 2 * tb)
    return pl.pallas_call(
        _combine_kernel,
        grid=(nb,),
        in_specs=[
            pl.BlockSpec((1, 1, 2 * tb), lambda i: (i, 0, 0), memory_space=pltpu.SMEM),
            pl.BlockSpec((1, 1, 2 * tb), lambda i: (jnp.minimum(i + 1, nb - 1), 0, 0), memory_space=pltpu.SMEM),
            pl.BlockSpec((tb, d), lambda i: (i, 0)),
            pl.BlockSpec((tb, LANES), lambda i: (i, 0)),
            pl.BlockSpec((1, d), lambda i: (0, 0)),
            pl.BlockSpec(memory_space=pl.ANY),
        ],
        out_specs=pl.BlockSpec((tb, d), lambda i: (i, 0)),
        out_shape=jax.ShapeDtypeStruct(x2.shape, F32),
        scratch_shapes=[pltpu.VMEM((2, 2, tb, d), F32), pltpu.SemaphoreType.DMA((2, 2))],
        compiler_params=pltpu.CompilerParams(dimension_semantics=("arbitrary",)),
        name="moe_combine",
    )(pos3, pos3, x2, gate_pad, final_w.reshape(1, d), y_rows)


def _moe_layer_and_final_norm(x2, norm_w, final_w, router_w, w_gate, w_up, w_down):
    t, d = x2.shape
    ne = MOE_EXPERTS
    tm = MOE_TILE
    idx_pad, gate_pad = _router(x2, norm_w, router_w)
    flat_e = idx_pad[:, :2].reshape(-1)
    onehot = (flat_e[:, None] == jnp.arange(ne, dtype=jnp.int32)[None, :]).astype(jnp.int32)
    csum = jnp.cumsum(onehot, axis=0)
    rank = jnp.sum(csum * onehot, axis=1) - 1
    counts = csum[-1]
    padded = ((counts + tm - 1) // tm) * tm
    ends = jnp.cumsum(padded)
    starts = ends - padded
    pos = (jnp.sum(starts[None, :] * onehot, axis=1) + rank).astype(jnp.int32)
    nt = (2 * t) // tm + ne
    tile_ids = jnp.arange(nt, dtype=jnp.int32)
    tile_valid = (tile_ids * tm < ends[-1]).astype(jnp.int32)
    tile_expert = jnp.minimum(jnp.sum((tile_ids[:, None] * tm >= ends[None, :]).astype(jnp.int32), axis=1), ne - 1)
    tile_src = tile_ids * tile_valid
    pad_info = jnp.concatenate([starts + counts, padded - counts, ends[-1:] // tm]).astype(jnp.int32)
    xs = _dispatch(x2, pos, pad_info, nt * tm)
    y_rows = _experts(xs, norm_w, tile_expert.astype(jnp.int32), tile_valid, tile_src, w_gate, w_up, w_down)
    return _combine(x2, final_w, pos, gate_pad, y_rows)


def kernel(x, mix_norm_w, ffn_norm_w, final_norm_w, hg_w_in, hg_lb_logits, hg_norm_w, hg_w_out, ssd_w_in, ssd_conv_w, ssd_conv_b, ssd_dt_bias, ssd_a_log, ssd_d, ssd_norm_w, ssd_w_out, ffn_w_gate, ffn_w_up, ffn_w_down, moe_router, moe_w_gate, moe_w_up, moe_w_down):
    bsz, seq, d = x.shape
    assert mix_norm_w.shape[0] == 2 and hg_w_in.shape[0] == 1 and ssd_w_in.shape[0] == 1
    x = _hgrn_layer(x, mix_norm_w[0], hg_w_in[0], hg_lb_logits, hg_norm_w[0], hg_w_out[0])
    x2 = _ffn_layer(x.reshape(bsz * seq, d), ffn_norm_w[0], ffn_w_gate[0], ffn_w_up[0], ffn_w_down[0])
    x = _ssd_layer(x2.reshape(bsz, seq, d), mix_norm_w[1], ssd_w_in[0], ssd_conv_w[0], ssd_conv_b[0],
                   ssd_dt_bias[0], ssd_a_log[0], ssd_d[0], ssd_norm_w[0], ssd_w_out[0])
    out = _moe_layer_and_final_norm(x.reshape(bsz * seq, d), ffn_norm_w[1], final_norm_w, moe_router[0],
                                    moe_w_gate[0], moe_w_up[0], moe_w_down[0])
    return out.reshape(bsz, seq, d)
```

```python
import functools

import jax
import jax.numpy as jnp
from jax import lax
from jax.experimental import pallas as pl
from jax.experimental.pallas import tpu as pltpu

F32 = jnp.float32
BF16 = jnp.bfloat16
EPS = 1e-6
LOG2_E = 1.4426950408889634
LANES = 128
VMEM_LIMIT = 60000 * 1024

HG_CHUNK = 256
HG_LEVELS = (128, 64, 32, 16, 8, 4, 2, 1)
HG_TOKENS = 256
SSD_CHUNK = 128
SSD_TOKENS = 256
SSD_HEADDIM = 64
SSD_STATE = 128
SSD_GROUPS = 4
CONV_W = 4
CONV_TAIL = 8
FFN_TOKENS = 1024
FFN_CHUNK = 2816
MOE_EXPERTS = 8
MOE_TILE = 512
ROUTER_TOKENS = 512
COMBINE_TOKENS = 256


def _dot(a, b):
    return jnp.dot(a, b, preferred_element_type=F32)


def _dot_nt(a, b):
    return lax.dot_general(a, b, (((1,), (1,)), ((), ())), preferred_element_type=F32)


def _dot_tn(a, b):
    return lax.dot_general(a, b, (((0,), (0,)), ((), ())), preferred_element_type=F32)


def _sigmoid(x):
    return 0.5 * jnp.tanh(0.5 * x) + 0.5


def _rms_rows(x, w):
    return x * lax.rsqrt(jnp.mean(x * x, axis=-1, keepdims=True) + EPS) * w


def _tri_ones(n):
    r = lax.broadcasted_iota(jnp.int32, (n, n), 0)
    c = lax.broadcasted_iota(jnp.int32, (n, n), 1)
    return jnp.where(r >= c, 1.0, 0.0).astype(BF16)


def _cumsum_rows(tri, x):
    hi = x.astype(BF16)
    r1 = x - hi.astype(F32)
    mid = r1.astype(BF16)
    lo = (r1 - mid.astype(F32)).astype(BF16)
    return _dot(tri, hi) + _dot(tri, mid) + _dot(tri, lo)


def _hgrn_kernel(x_ref, mw_ref, win_ref, lbl_ref, nw_ref, wout_ref, o_ref, proj_s, st_s, *, nh, dk, dv):
    tc = x_ref.shape[1]
    qk = nh * dk
    vd = nh * dv

    @pl.when(pl.program_id(1) == 0)
    def _():
        st_s[...] = jnp.zeros_like(st_s)

    x = x_ref[0]
    h = _rms_rows(x, mw_ref[...]).astype(BF16)
    for r0 in range(0, tc, HG_CHUNK):
        proj_s[r0:r0 + HG_CHUNK, :] = _dot(h[r0:r0 + HG_CHUNK], win_ref[...])

    lbl = lbl_ref[...]
    lmax = jnp.max(lbl, axis=0, keepdims=True)
    le = jnp.exp(lbl - lmax)
    lb = le[0:1, :] / jnp.sum(le, axis=0, keepdims=True)

    tri = _tri_ones(HG_CHUNK)
    ti = lax.broadcasted_iota(jnp.int32, (HG_CHUNK, HG_CHUNK), 0)
    si = lax.broadcasted_iota(jnp.int32, (HG_CHUNK, HG_CHUNK), 1)
    level = jnp.where(si > ti, -2, 31 - lax.clz(ti ^ si))
    level_ids = [hs.bit_length() - 1 for hs in HG_LEVELS] + [-1]
    sub8 = lax.broadcasted_iota(jnp.int32, (HG_CHUNK // 8, 8, qk), 1)

    def boundary_rows(b, hs):
        if hs >= 8:
            parts = [jnp.broadcast_to(b[lo + hs - 1:lo + hs, :], (2 * hs, qk)) for lo in range(0, HG_CHUNK, 2 * hs)]
            return parts[0] if len(parts) == 1 else jnp.concatenate(parts, axis=0)
        b3 = b.reshape(HG_CHUNK // 8, 8, qk)
        ref = jnp.broadcast_to(b3[:, hs - 1:hs, :], b3.shape)
        for lo in range(2 * hs, 8, 2 * hs):
            ref = jnp.where(sub8 >= lo, jnp.broadcast_to(b3[:, lo + hs - 1:lo + hs, :], b3.shape), ref)
        return ref.reshape(HG_CHUNK, qk)

    def chunk_body(c):
        r0 = c * HG_CHUNK
        q = proj_s[pl.ds(r0, HG_CHUNK), 0:qk]
        fp = proj_s[pl.ds(r0, HG_CHUNK), qk:2 * qk]
        v = proj_s[pl.ds(r0, HG_CHUNK), 2 * qk:2 * qk + vd]
        f = lb + (1.0 - lb) * _sigmoid(fp)
        k = 1.0 - f
        b = _cumsum_rows(tri, jnp.log2(f))
        b_last = b[HG_CHUNK - 1:HG_CHUNK, :]
        qg = (q * jnp.exp2(b)).astype(BF16)
        kdec = (k * jnp.exp2(b_last - b)).astype(BF16)
        g_last = jnp.exp2(b_last)
        vb = v.astype(BF16)

        qds = []
        kds = []
        for hs in HG_LEVELS:
            z = jnp.exp2(-jnp.abs(b - boundary_rows(b, hs)))
            qds.append((q * z).astype(BF16))
            kds.append((k * z).astype(BF16))
        qds.append(q.astype(BF16))
        kds.append(k.astype(BF16))

        o_heads = []
        for hh in range(nh):
            ks = slice(hh * dk, (hh + 1) * dk)
            attn = jnp.zeros((HG_CHUNK, HG_CHUNK), F32)
            for qd, kd, lid in zip(qds, kds, level_ids):
                attn = jnp.where(level == lid, _dot_nt(qd[:, ks], kd[:, ks]), attn)
            st = st_s[hh]
            o_inter = _dot_nt(qg[:, ks], st.astype(BF16))
            oh = o_inter + _dot(attn.astype(BF16), vb[:, hh * dv:(hh + 1) * dv])
            o_heads.append(oh * lax.rsqrt(jnp.mean(oh * oh, axis=-1, keepdims=True) + EPS))
            st_s[hh] = st * g_last[:, ks] + _dot_tn(vb[:, hh * dv:(hh + 1) * dv], kdec[:, ks])
        g = proj_s[pl.ds(r0, HG_CHUNK), 2 * qk + vd:2 * qk + 2 * vd]
        on = jnp.concatenate(o_heads, axis=1) * nw_ref[...] * _sigmoid(g)
        o_ref[0, pl.ds(r0, HG_CHUNK), :] = x[r0:r0 + HG_CHUNK] + _dot(on.astype(BF16), wout_ref[...])

    for c in range(tc // HG_CHUNK):
        chunk_body(c)


def _hgrn_layer(x, mix_w, w_in, lb_logits, norm_w, w_out):
    bsz, seq, d = x.shape
    dv = norm_w.shape[0]
    nh = w_out.shape[0] // dv
    dk = (w_in.shape[1] - 2 * nh * dv) // (2 * nh)
    assert lb_logits.shape[0] == 2 and dk == LANES and dv == LANES
    tc = HG_TOKENS
    assert seq % tc == 0 and tc % HG_CHUNK == 0
    const = lambda b, j: (0, 0)
    return pl.pallas_call(
        functools.partial(_hgrn_kernel, nh=nh, dk=dk, dv=dv),
        grid=(bsz, seq // tc),
        in_specs=[
            pl.BlockSpec((1, tc, d), lambda b, j: (b, j, 0)),
            pl.BlockSpec((1, d), const),
            pl.BlockSpec(w_in.shape, const, pipeline_mode=pl.Buffered(1)),
            pl.BlockSpec(lb_logits.shape, const),
            pl.BlockSpec((1, nh * dv), const),
            pl.BlockSpec(w_out.shape, const, pipeline_mode=pl.Buffered(1)),
        ],
        out_specs=pl.BlockSpec((1, tc, d), lambda b, j: (b, j, 0)),
        out_shape=jax.ShapeDtypeStruct(x.shape, F32),
        scratch_shapes=[
            pltpu.VMEM((tc, w_in.shape[1]), F32),
            pltpu.VMEM((nh, dv, dk), F32),
        ],
        compiler_params=pltpu.CompilerParams(
            dimension_semantics=("arbitrary", "arbitrary"), vmem_limit_bytes=VMEM_LIMIT),
        name="hgrn_mixer",
    )(x, mix_w.reshape(1, d), w_in.astype(BF16), lb_logits, jnp.tile(norm_w, nh).reshape(1, nh * dv),
      w_out.astype(BF16))


def _swiglu_rows(h, wg_ref, wu_ref, wd_ref, acc):
    dff = wg_ref.shape[-1]
    for c0 in range(0, dff, FFN_CHUNK):
        g = _dot(h, wg_ref[:, c0:c0 + FFN_CHUNK])
        u = _dot(h, wu_ref[:, c0:c0 + FFN_CHUNK])
        a = (g * _sigmoid(g) * u).astype(BF16)
        y = _dot(a, wd_ref[c0:c0 + FFN_CHUNK, :])
        acc = y if acc is None else acc + y
    return acc


def _ffn_kernel(x_ref, nw_ref, wg_ref, wu_ref, wd_ref, o_ref):
    x = x_ref[...]
    h = _rms_rows(x, nw_ref[...]).astype(BF16)
    o_ref[...] = _swiglu_rows(h, wg_ref, wu_ref, wd_ref, x)


def _ffn_layer(x2, norm_w, w_gate, w_up, w_down):
    t, d = x2.shape
    dff = w_gate.shape[1]
    assert t % FFN_TOKENS == 0 and dff % FFN_CHUNK == 0
    const = lambda i: (0, 0)
    return pl.pallas_call(
        _ffn_kernel,
        grid=(t // FFN_TOKENS,),
        in_specs=[
            pl.BlockSpec((FFN_TOKENS, d), lambda i: (i, 0)),
            pl.BlockSpec((1, d), const),
            pl.BlockSpec((d, dff), const, pipeline_mode=pl.Buffered(1)),
            pl.BlockSpec((d, dff), const, pipeline_mode=pl.Buffered(1)),
            pl.BlockSpec((dff, d), const, pipeline_mode=pl.Buffered(1)),
        ],
        out_specs=pl.BlockSpec((FFN_TOKENS, d), lambda i: (i, 0)),
        out_shape=jax.ShapeDtypeStruct(x2.shape, F32),
        compiler_params=pltpu.CompilerParams(
            dimension_semantics=("arbitrary",), vmem_limit_bytes=VMEM_LIMIT),
        name="dense_ffn",
    )(x2, norm_w.reshape(1, d), w_gate.astype(BF16), w_up.astype(BF16), w_down.astype(BF16))


def _ssd_kernel(x_ref, mw_ref, win_ref, cw_ref, cb_ref, dtb_ref, alog_ref, dsk_ref, nw_ref, wout_ref, o_ref,
                proj_s, cbuf, y_s, st_s, *, d_inner):
    tc = x_ref.shape[1]
    p2 = 2 * SSD_HEADDIM
    assert p2 == LANES and SSD_STATE == LANES
    gs = SSD_GROUPS * SSD_STATE
    cd = d_inner + 2 * gs
    pairs_per_group = d_inner // (SSD_GROUPS * p2)
    q = SSD_CHUNK
    assert q == LANES

    @pl.when(pl.program_id(1) == 0)
    def _():
        st_s[...] = jnp.zeros_like(st_s)
        cbuf[:, 0:CONV_TAIL, :] = jnp.zeros((cd // LANES, CONV_TAIL, LANES), F32)

    x = x_ref[0]
    h = _rms_rows(x, mw_ref[...]).astype(BF16)
    proj_s[...] = _dot(h, win_ref[...])

    for j in range(cd // LANES):
        c0 = d_inner + j * LANES
        cbuf[j, CONV_TAIL:CONV_TAIL + tc, :] = proj_s[:, c0:c0 + LANES]
        conv = cb_ref[:, j * LANES:(j + 1) * LANES]
        for k in range(CONV_W):
            off = CONV_TAIL - (CONV_W - 1) + k
            conv = conv + cw_ref[k:k + 1, j * LANES:(j + 1) * LANES] * cbuf[j, off:off + tc, :]
        cbuf[j, 0:CONV_TAIL, :] = cbuf[j, tc:tc + CONV_TAIL, :]
        proj_s[:, c0:c0 + LANES] = conv * _sigmoid(conv)

    dt_pre = proj_s[:, d_inner + cd:d_inner + cd + LANES] + dtb_ref[...]
    dt = jnp.maximum(dt_pre, 0.0) + jnp.log1p(jnp.exp(-jnp.abs(dt_pre)))
    da = dt * (-LOG2_E * jnp.exp(alog_ref[...]))

    tri = _tri_ones(q)
    rr = lax.broadcasted_iota(jnp.int32, (q, q), 0)
    cc = lax.broadcasted_iota(jnp.int32, (q, q), 1)
    causal = rr >= cc
    first = lax.broadcasted_iota(jnp.int32, (1, LANES), 1) < SSD_HEADDIM

    def lane_col(a, hd):
        return jnp.broadcast_to(a[:, hd:hd + 1], (q, LANES))

    for sc in range(tc // q):
        lo, hi = sc * q, (sc + 1) * q
        dtc = dt[lo:hi]
        cs = _cumsum_rows(tri, da[lo:hi])
        last = cs[q - 1:q, :]
        cs_t = cs.T
        for g in range(SSD_GROUPS):
            bm = proj_s[lo:hi, 2 * d_inner + g * SSD_STATE:2 * d_inner + (g + 1) * SSD_STATE]
            cm = proj_s[lo:hi, 2 * d_inner + gs + g * SSD_STATE:2 * d_inner + gs + (g + 1) * SSD_STATE]
            bmb = bm.astype(BF16)
            cmb = cm.astype(BF16)
            cb = _dot_nt(cmb, bmb)
            for pr in range(pairs_per_group):
                pi = g * pairs_per_group + pr
                h1, h2 = 2 * pi, 2 * pi + 1
                xp = proj_s[lo:hi, d_inner + pi * p2:d_inner + (pi + 1) * p2]
                c1 = lane_col(cs, h1)
                c2 = lane_col(cs, h2)
                cs_p = jnp.where(first, c1, c2)
                last_p = jnp.where(first, last[:, h1:h1 + 1], last[:, h2:h2 + 1])
                xdt = jnp.where(first, lane_col(dtc, h1), lane_col(dtc, h2)) * xp
                m1 = cb * jnp.exp2(jnp.where(causal, c1 - cs_t[h1:h1 + 1, :], -jnp.inf))
                m2 = cb * jnp.exp2(jnp.where(causal, c2 - cs_t[h2:h2 + 1, :], -jnp.inf))
                x1 = jnp.where(first, xdt, 0.0).astype(BF16)
                x2 = jnp.where(first, 0.0, xdt).astype(BF16)
                y = _dot(m1.astype(BF16), x1) + _dot(m2.astype(BF16), x2)
                st = st_s[pi]
                y = y + jnp.exp2(cs_p) * _dot(cmb, st.astype(BF16))
                wx = (jnp.exp2(last_p - cs_p) * xdt).astype(BF16)
                st_s[pi] = st * jnp.exp2(last_p) + _dot_tn(bmb, wx)
                y_s[lo:hi, pi * p2:(pi + 1) * p2] = y + dsk_ref[:, pi * p2:(pi + 1) * p2] * xp

    z = proj_s[:, 0:d_inner]
    y = y_s[...] * (z * _sigmoid(z))
    gw = d_inner // SSD_GROUPS
    parts = []
    for g in range(SSD_GROUPS):
        yg = y[:, g * gw:(g + 1) * gw]
        parts.append(yg * lax.rsqrt(jnp.mean(yg * yg, axis=-1, keepdims=True) + EPS))
    yn = jnp.concatenate(parts, axis=1) * nw_ref[...]
    o_ref[0] = x + _dot(yn.astype(BF16), wout_ref[...])


def _ssd_layer(x, mix_w, w_in, conv_w, conv_b, dt_bias, a_log, d_skip, norm_w, w_out):
    bsz, seq, d = x.shape
    d_inner = w_out.shape[0]
    nheads = dt_bias.shape[0]
    assert d_inner == nheads * SSD_HEADDIM and nheads <= LANES
    cd = d_inner + 2 * SSD_GROUPS * SSD_STATE
    assert conv_w.shape == (CONV_W, cd) and w_in.shape[1] == d_inner + cd + nheads
    tc = SSD_TOKENS
    assert seq % tc == 0 and tc % SSD_CHUNK == 0
    pad = LANES - nheads
    w_all = jnp.pad(w_in, ((0, 0), (0, pad))).astype(BF16)
    n_all = w_all.shape[1]
    const = lambda b, j: (0, 0)
    return pl.pallas_call(
        functools.partial(_ssd_kernel, d_inner=d_inner),
        grid=(bsz, seq // tc),
        in_specs=[
            pl.BlockSpec((1, tc, d), lambda b, j: (b, j, 0)),
            pl.BlockSpec((1, d), const),
            pl.BlockSpec(w_all.shape, const, pipeline_mode=pl.Buffered(1)),
            pl.BlockSpec((CONV_W, cd), const),
            pl.BlockSpec((1, cd), const),
            pl.BlockSpec((1, LANES), const),
            pl.BlockSpec((1, LANES), const),
            pl.BlockSpec((1, d_inner), const),
            pl.BlockSpec((1, d_inner), const),
            pl.BlockSpec(w_out.shape, const, pipeline_mode=pl.Buffered(1)),
        ],
        out_specs=pl.BlockSpec((1, tc, d), lambda b, j: (b, j, 0)),
        out_shape=jax.ShapeDtypeStruct(x.shape, F32),
        scratch_shapes=[
            pltpu.VMEM((tc, n_all), F32),
            pltpu.VMEM((cd // LANES, tc + CONV_TAIL, LANES), F32),
            pltpu.VMEM((tc, d_inner), F32),
            pltpu.VMEM((nheads // 2, SSD_STATE, LANES), F32),
        ],
        compiler_params=pltpu.CompilerParams(
            dimension_semantics=("arbitrary", "arbitrary"), vmem_limit_bytes=VMEM_LIMIT),
        name="ssd_mixer",
    )(x, mix_w.reshape(1, d), w_all, conv_w, conv_b.reshape(1, cd),
      jnp.pad(dt_bias, (0, pad)).reshape(1, LANES), jnp.pad(a_log, (0, pad)).reshape(1, LANES),
      jnp.repeat(d_skip, SSD_HEADDIM).reshape(1, d_inner), norm_w.reshape(1, d_inner), w_out.astype(BF16))


def _router_kernel(x_ref, nw_ref, rw_ref, idx_ref, gate_ref):
    h = _rms_rows(x_ref[...], nw_ref[...])
    hi = h.astype(BF16)
    lo = (h - hi.astype(F32)).astype(BF16)
    both = _dot(hi, rw_ref[...])
    logits = both[:, :LANES] + both[:, LANES:] + _dot(lo, rw_ref[:, :LANES])
    lane = lax.broadcasted_iota(jnp.int32, logits.shape, 1)
    valid = lane < MOE_EXPERTS
    lg = jnp.where(valid, logits, -jnp.inf)
    e = jnp.exp(lg - jnp.max(lg, axis=-1, keepdims=True))
    p = jnp.where(valid, e / jnp.sum(e, axis=-1, keepdims=True), -1.0)
    m1 = jnp.max(p, axis=-1, keepdims=True)
    i1 = jnp.min(jnp.where(p == m1, lane, LANES), axis=-1, keepdims=True)
    p2 = jnp.where(lane == i1, -1.0, p)
    m2 = jnp.max(p2, axis=-1, keepdims=True)
    i2 = jnp.min(jnp.where(p2 == m2, lane, LANES), axis=-1, keepdims=True)
    s = m1 + m2
    idx_ref[...] = jnp.where(lane == 0, i1, jnp.where(lane == 1, i2, 0))
    gate_ref[...] = jnp.where(lane == 0, m1 / s, jnp.where(lane == 1, m2 / s, 0.0))


def _router(x2, norm_w, router_w):
    t, d = x2.shape
    ne = router_w.shape[1]
    assert ne == MOE_EXPERTS and t % ROUTER_TOKENS == 0
    rw = jnp.pad(router_w, ((0, 0), (0, LANES - ne)))
    rw_hi = rw.astype(BF16)
    rw_lo = (rw - rw_hi.astype(F32)).astype(BF16)
    rw = jnp.concatenate([rw_hi, rw_lo], axis=1)
    return pl.pallas_call(
        _router_kernel,
        grid=(t // ROUTER_TOKENS,),
        in_specs=[
            pl.BlockSpec((ROUTER_TOKENS, d), lambda i: (i, 0)),
            pl.BlockSpec((1, d), lambda i: (0, 0)),
            pl.BlockSpec((d, 2 * LANES), lambda i: (0, 0)),
        ],
        out_specs=[pl.BlockSpec((ROUTER_TOKENS, LANES), lambda i: (i, 0))] * 2,
        out_shape=[jax.ShapeDtypeStruct((t, LANES), jnp.int32), jax.ShapeDtypeStruct((t, LANES), F32)],
        compiler_params=pltpu.CompilerParams(dimension_semantics=("arbitrary",)),
        name="moe_router",
    )(x2, norm_w.reshape(1, d), rw)


def _row_gather_start(src_hbm, dst, row_of, n, sem):
    for r in range(n):
        pltpu.make_async_copy(src_hbm.at[pl.ds(row_of(r), 1), :], dst.at[pl.ds(r, 1), :], sem).start()


def _row_gather_wait(src_hbm, dst, n, sem):
    pltpu.make_async_copy(src_hbm.at[pl.ds(0, n), :], dst, sem).wait()


def _dispatch_kernel(pad_ref, pos_ref, x_ref, xs_hbm, xbuf, zbuf, sem, pad_sem):
    i = pl.program_id(0)
    last = pl.num_programs(0) - 1
    tb = x_ref.shape[0]
    slot = lax.rem(i, 2)

    xbuf[slot] = x_ref[...]

    def issue(r, carry):
        src = xbuf.at[slot, pl.ds(r, 1), :]
        pltpu.make_async_copy(src, xs_hbm.at[pl.ds(pos_ref[0, 0, 2 * r], 1), :], sem.at[slot]).start()
        pltpu.make_async_copy(src, xs_hbm.at[pl.ds(pos_ref[0, 0, 2 * r + 1], 1), :], sem.at[slot]).start()
        return carry

    for r in range(tb):
        issue(r, 0)

    def wait_slot(s):
        for _ in range(2):
            pltpu.make_async_copy(xbuf.at[s], xs_hbm.at[pl.ds(0, tb), :], sem.at[s]).wait()

    @pl.when(i > 0)
    def _():
        wait_slot(1 - slot)

    @pl.when(i == last)
    def _():
        wait_slot(slot)
        zbuf[...] = jnp.zeros_like(zbuf)
        for e in range(MOE_EXPERTS):
            first = pad_ref[e]
            count = pad_ref[MOE_EXPERTS + e]

            def zero_issue(r, carry, first=first):
                pltpu.make_async_copy(zbuf.at[pl.ds(0, 1), :], xs_hbm.at[pl.ds(first + r, 1), :], pad_sem).start()
                return carry

            def zero_wait(r, carry, first=first):
                pltpu.make_async_copy(zbuf.at[pl.ds(0, 1), :], xs_hbm.at[pl.ds(first, 1), :], pad_sem).wait()
                return carry

            lax.fori_loop(0, count, zero_issue, 0)
            lax.fori_loop(0, count, zero_wait, 0)

        tm = zbuf.shape[0]
        first_tile = pad_ref[2 * MOE_EXPERTS]

        def tile_copy(j):
            return pltpu.make_async_copy(zbuf, xs_hbm.at[pl.ds((first_tile + j) * tm, tm), :], pad_sem)

        def tile_issue(j, carry):
            tile_copy(j).start()
            return carry

        def tile_wait(j, carry):
            tile_copy(j).wait()
            return carry

        n_unused = xs_hbm.shape[0] // tm - first_tile
        lax.fori_loop(0, n_unused, tile_issue, 0)
        lax.fori_loop(0, n_unused, tile_wait, 0)


def _dispatch(x2, pos, pad_info, n_rows):
    t, d = x2.shape
    tb = COMBINE_TOKENS
    assert t % tb == 0
    grid_spec = pltpu.PrefetchScalarGridSpec(
        num_scalar_prefetch=1,
        grid=(t // tb,),
        in_specs=[
            pl.BlockSpec((1, 1, 2 * tb), lambda i, pad: (i, 0, 0), memory_space=pltpu.SMEM),
            pl.BlockSpec((tb, d), lambda i, pad: (i, 0)),
        ],
        out_specs=pl.BlockSpec(memory_space=pl.ANY),
        scratch_shapes=[pltpu.VMEM((2, tb, d), F32), pltpu.VMEM((MOE_TILE, d), F32),
                        pltpu.SemaphoreType.DMA((2,)), pltpu.SemaphoreType.DMA(())],
    )
    return pl.pallas_call(
        _dispatch_kernel,
        grid_spec=grid_spec,
        out_shape=jax.ShapeDtypeStruct((n_rows, d), F32),
        compiler_params=pltpu.CompilerParams(dimension_semantics=("arbitrary",)),
        name="moe_dispatch",
    )(pad_info, pos.reshape(t // tb, 1, 2 * tb), x2)


def _experts_kernel(te_ref, tv_ref, ts_ref, xs_ref, nw_ref, wg_ref, wu_ref, wd_ref, y_ref):
    i = pl.program_id(0)

    @pl.when(tv_ref[i] > 0)
    def _():
        h = _rms_rows(xs_ref[...], nw_ref[...]).astype(BF16)
        y_ref[...] = _swiglu_rows(h, wg_ref.at[0], wu_ref.at[0], wd_ref.at[0], None)

    @pl.when(tv_ref[i] == 0)
    def _():
        y_ref[...] = jnp.zeros_like(y_ref)


def _experts(xs, norm_w, tile_expert, tile_valid, tile_src, w_gate, w_up, w_down):
    n_rows, d = xs.shape
    ne, _, dff = w_gate.shape
    tm = MOE_TILE
    nt = n_rows // tm
    wmap = lambda i, te, tv, ts: (te[i], 0, 0)
    grid_spec = pltpu.PrefetchScalarGridSpec(
        num_scalar_prefetch=3,
        grid=(nt,),
        in_specs=[
            pl.BlockSpec((tm, d), lambda i, te, tv, ts: (ts[i], 0)),
            pl.BlockSpec((1, d), lambda i, te, tv, ts: (0, 0)),
            pl.BlockSpec((1, d, dff), wmap),
            pl.BlockSpec((1, d, dff), wmap),
            pl.BlockSpec((1, dff, d), wmap),
        ],
        out_specs=pl.BlockSpec((tm, d), lambda i, te, tv, ts: (i, 0)),
    )
    return pl.pallas_call(
        _experts_kernel,
        grid_spec=grid_spec,
        out_shape=jax.ShapeDtypeStruct((n_rows, d), F32),
        compiler_params=pltpu.CompilerParams(
            dimension_semantics=("arbitrary",), vmem_limit_bytes=VMEM_LIMIT),
        name="moe_experts",
    )(tile_expert, tile_valid, tile_src, xs, norm_w.reshape(1, d),
      w_gate.astype(BF16), w_up.astype(BF16), w_down.astype(BF16))


def _combine_kernel(pos_ref, posn_ref, x_ref, gate_ref, fw_ref, y_hbm, o_ref, ybuf, sem):
    i = pl.program_id(0)
    tb = x_ref.shape[0]
    slot = lax.rem(i, 2)

    def start(p_ref, s):
        for j in range(2):
            _row_gather_start(y_hbm, ybuf.at[s, j], lambda r, j=j: p_ref[0, 0, 2 * r + j], tb, sem.at[s, j])

    @pl.when(i == 0)
    def _():
        start(pos_ref, 0)

    @pl.when(i + 1 < pl.num_programs(0))
    def _():
        start(posn_ref, 1 - slot)

    for j in range(2):
        _row_gather_wait(y_hbm, ybuf.at[slot, j], tb, sem.at[slot, j])
    g = gate_ref[...]
    o = x_ref[...] + g[:, 0:1] * ybuf[slot, 0] + g[:, 1:2] * ybuf[slot, 1]
    o_ref[...] = _rms_rows(o, fw_ref[...])


def _combine(x2, final_w, pos, gate_pad, y_rows):
    t, d = x2.shape
    tb = COMBINE_TOKENS
    nb = t // tb
    pos3 = pos.reshape(nb, 1, 2 * tb)
    return pl.pallas_call(
        _combine_kernel,
        grid=(nb,),
        in_specs=[
            pl.BlockSpec((1, 1, 2 * tb), lambda i: (i, 0, 0), memory_space=pltpu.SMEM),
            pl.BlockSpec((1, 1, 2 * tb), lambda i: (jnp.minimum(i + 1, nb - 1), 0, 0), memory_space=pltpu.SMEM),
            pl.BlockSpec((tb, d), lambda i: (i, 0)),
            pl.BlockSpec((tb, LANES), lambda i: (i, 0)),
            pl.BlockSpec((1, d), lambda i: (0, 0)),
            pl.BlockSpec(memory_space=pl.ANY),
        ],
        out_specs=pl.BlockSpec((tb, d), lambda i: (i, 0)),
        out_shape=jax.ShapeDtypeStruct(x2.shape, F32),
        scratch_shapes=[pltpu.VMEM((2, 2, tb, d), F32), pltpu.SemaphoreType.DMA((2, 2))],
        compiler_params=pltpu.CompilerParams(dimension_semantics=("arbitrary",)),
        name="moe_combine",
    )(pos3, pos3, x2, gate_pad, final_w.reshape(1, d), y_rows)


def _moe_layer_and_final_norm(x2, norm_w, final_w, router_w, w_gate, w_up, w_down):
    t, d = x2.shape
    ne = MOE_EXPERTS
    tm = MOE_TILE
    idx_pad, gate_pad = _router(x2, norm_w, router_w)
    flat_e = idx_pad[:, :2].reshape(-1)
    onehot = (flat_e[:, None] == jnp.arange(ne, dtype=jnp.int32)[None, :]).astype(jnp.int32)
    csum = jnp.cumsum(onehot, axis=0)
    rank = jnp.sum(csum * onehot, axis=1) - 1
    counts = csum[-1]
    padded = ((counts + tm - 1) // tm) * tm
    ends = jnp.cumsum(padded)
    starts = ends - padded
    pos = (jnp.sum(starts[None, :] * onehot, axis=1) + rank).astype(jnp.int32)
    nt = (2 * t) // tm + ne
    tile_ids = jnp.arange(nt, dtype=jnp.int32)
    tile_valid = (tile_ids * tm < ends[-1]).astype(jnp.int32)
    tile_expert = jnp.minimum(jnp.sum((tile_ids[:, None] * tm >= ends[None, :]).astype(jnp.int32), axis=1), ne - 1)
    tile_src = tile_ids * tile_valid
    pad_info = jnp.concatenate([starts + counts, padded - counts, ends[-1:] // tm]).astype(jnp.int32)
    xs = _dispatch(x2, pos, pad_info, nt * tm)
    y_rows = _experts(xs, norm_w, tile_expert.astype(jnp.int32), tile_valid, tile_src, w_gate, w_up, w_down)
    return _combine(x2, final_w, pos, gate_pad, y_rows)


def kernel(x, mix_norm_w, ffn_norm_w, final_norm_w, hg_w_in, hg_lb_logits, hg_norm_w, hg_w_out, ssd_w_in, ssd_conv_w, ssd_conv_b, ssd_dt_bias, ssd_a_log, ssd_d, ssd_norm_w, ssd_w_out, ffn_w_gate, ffn_w_up, ffn_w_down, moe_router, moe_w_gate, moe_w_up, moe_w_down):
    bsz, seq, d = x.shape
    assert mix_norm_w.shape[0] == 2 and hg_w_in.shape[0] == 1 and ssd_w_in.shape[0] == 1
    x = _hgrn_layer(x, mix_norm_w[0], hg_w_in[0], hg_lb_logits, hg_norm_w[0], hg_w_out[0])
    x2 = _ffn_layer(x.reshape(bsz * seq, d), ffn_norm_w[0], ffn_w_gate[0], ffn_w_up[0], ffn_w_down[0])
    x = _ssd_layer(x2.reshape(bsz, seq, d), mix_norm_w[1], ssd_w_in[0], ssd_conv_w[0], ssd_conv_b[0],
                   ssd_dt_bias[0], ssd_a_log[0], ssd_d[0], ssd_norm_w[0], ssd_w_out[0])
    out = _moe_layer_and_final_norm(x.reshape(bsz * seq, d), ffn_norm_w[1], final_norm_w, moe_router[0],
                                    moe_w_gate[0], moe_w_up[0], moe_w_down[0])
    return out.reshape(bsz, seq, d)
```

```python
import functools

import jax
import jax.numpy as jnp
from jax import lax
from jax.experimental import pallas as pl
from jax.experimental.pallas import tpu as pltpu

F32 = jnp.float32
BF16 = jnp.bfloat16
EPS = 1e-6
LOG2_E = 1.4426950408889634
LANES = 128
VMEM_LIMIT = 60000 * 1024

HG_CHUNK = 256
HG_LEVELS = (128, 64, 32, 16, 8, 4, 2, 1)
HG_TOKENS = 256
SSD_CHUNK = 128
SSD_BLOCK = 256
SSD_PROJ_SLAB = 256
SSD_HEADDIM = 64
SSD_STATE = 128
SSD_GROUPS = 4
CONV_W = 4
CONV_TAIL = 8
FFN_TOKENS = 1024
FFN_CHUNK = 2816
MOE_EXPERTS = 8
MOE_TILE = 512
ROUTER_TOKENS = 512
COMBINE_TOKENS = 256


def _dot(a, b):
    return jnp.dot(a, b, preferred_element_type=F32)


def _dot_nt(a, b):
    return lax.dot_general(a, b, (((1,), (1,)), ((), ())), preferred_element_type=F32)


def _dot_tn(a, b):
    return lax.dot_general(a, b, (((0,), (0,)), ((), ())), preferred_element_type=F32)


def _sigmoid(x):
    return 0.5 * jnp.tanh(0.5 * x) + 0.5


def _rms_rows(x, w):
    return x * lax.rsqrt(jnp.mean(x * x, axis=-1, keepdims=True) + EPS) * w


def _tri_ones(n):
    r = lax.broadcasted_iota(jnp.int32, (n, n), 0)
    c = lax.broadcasted_iota(jnp.int32, (n, n), 1)
    return jnp.where(r >= c, 1.0, 0.0).astype(BF16)


def _cumsum_rows(tri, x):
    hi = x.astype(BF16)
    r1 = x - hi.astype(F32)
    mid = r1.astype(BF16)
    lo = (r1 - mid.astype(F32)).astype(BF16)
    return _dot(tri, hi) + _dot(tri, mid) + _dot(tri, lo)


def _hgrn_kernel(x_ref, mw_ref, win_ref, lbl_ref, nw_ref, wout_ref, o_ref, proj_s, st_s, *, nh, dk, dv):
    tc = x_ref.shape[1]
    qk = nh * dk
    vd = nh * dv

    @pl.when(pl.program_id(1) == 0)
    def _():
        st_s[...] = jnp.zeros_like(st_s)

    x = x_ref[0]
    h = _rms_rows(x, mw_ref[...]).astype(BF16)
    for r0 in range(0, tc, HG_CHUNK):
        proj_s[r0:r0 + HG_CHUNK, :] = _dot(h[r0:r0 + HG_CHUNK], win_ref[...])

    lbl = lbl_ref[...]
    lmax = jnp.max(lbl, axis=0, keepdims=True)
    le = jnp.exp(lbl - lmax)
    lb = le[0:1, :] / jnp.sum(le, axis=0, keepdims=True)

    tri = _tri_ones(HG_CHUNK)
    half = HG_CHUNK // 2
    assert HG_LEVELS[0] == half
    ti = lax.broadcasted_iota(jnp.int32, (half, half), 0)
    si = lax.broadcasted_iota(jnp.int32, (half, half), 1)
    level = jnp.where(si > ti, -2, 31 - lax.clz(ti ^ si))
    level_ids = [hs.bit_length() - 1 for hs in HG_LEVELS[1:]] + [-1]
    sub8 = lax.broadcasted_iota(jnp.int32, (HG_CHUNK // 8, 8, qk), 1)

    def level_decay(b, hs):
        if hs >= 8:
            parts = []
            for lo in range(0, HG_CHUNK, 2 * hs):
                r = b[lo + hs - 1:lo + hs, :]
                parts += [r - b[lo:lo + hs], b[lo + hs:lo + 2 * hs] - r]
            return jnp.exp2(jnp.concatenate(parts, axis=0)).astype(BF16)
        b3 = b.reshape(HG_CHUNK // 8, 8, qk)
        ref = jnp.broadcast_to(b3[:, hs - 1:hs, :], b3.shape)
        for lo in range(2 * hs, 8, 2 * hs):
            ref = jnp.where(sub8 >= lo, jnp.broadcast_to(b3[:, lo + hs - 1:lo + hs, :], b3.shape), ref)
        return jnp.exp2(-jnp.abs(b - ref.reshape(HG_CHUNK, qk))).astype(BF16)

    def chunk_body(c):
        r0 = c * HG_CHUNK
        q = proj_s[pl.ds(r0, HG_CHUNK), 0:qk]
        fp = proj_s[pl.ds(r0, HG_CHUNK), qk:2 * qk]
        v = proj_s[pl.ds(r0, HG_CHUNK), 2 * qk:2 * qk + vd]
        f = lb + (1.0 - lb) * _sigmoid(fp)
        k = 1.0 - f
        b = _cumsum_rows(tri, jnp.log2(f))
        b_last = b[HG_CHUNK - 1:HG_CHUNK, :]
        qg = (q * jnp.exp2(b)).astype(BF16)
        kdec = (k * jnp.exp2(b_last - b)).astype(BF16)
        g_last = jnp.exp2(b_last)
        vb = v.astype(BF16)

        qb = q.astype(BF16)
        kb = k.astype(BF16)
        z_top = level_decay(b, half)
        qd_top = qb[half:] * z_top[half:]
        kd_top = kb[:half] * z_top[:half]
        qds = []
        kds = []
        for hs in HG_LEVELS[1:]:
            z = level_decay(b, hs)
            qds.append(qb * z)
            kds.append(kb * z)
        qds.append(qb)
        kds.append(kb)

        def diagonal_block(rows, ks):
            attn = jnp.zeros((half, half), F32)
            for qd, kd, lid in zip(qds, kds, level_ids):
                attn = jnp.where(level == lid, _dot_nt(qd[rows, ks], kd[rows, ks]), attn)
            return attn

        o_heads = []
        for hh in range(nh):
            ks = slice(hh * dk, (hh + 1) * dk)
            vh = vb[:, hh * dv:(hh + 1) * dv]
            attn_lo = diagonal_block(slice(0, half), ks)
            attn_hi = jnp.concatenate([_dot_nt(qd_top[:, ks], kd_top[:, ks]), diagonal_block(slice(half, HG_CHUNK), ks)],
                                      axis=1)
            o_intra = jnp.concatenate([_dot(attn_lo.astype(BF16), vh[:half]), _dot(attn_hi.astype(BF16), vh)], axis=0)
            st = st_s[hh]
            o_inter = _dot_nt(qg[:, ks], st.astype(BF16))
            oh = o_inter + o_intra
            o_heads.append(oh * lax.rsqrt(jnp.mean(oh * oh, axis=-1, keepdims=True) + EPS))
            st_s[hh] = st * g_last[:, ks] + _dot_tn(vb[:, hh * dv:(hh + 1) * dv], kdec[:, ks])
        g = proj_s[pl.ds(r0, HG_CHUNK), 2 * qk + vd:2 * qk + 2 * vd]
        on = jnp.concatenate(o_heads, axis=1) * nw_ref[...] * _sigmoid(g)
        o_ref[0, pl.ds(r0, HG_CHUNK), :] = x[r0:r0 + HG_CHUNK] + _dot(on.astype(BF16), wout_ref[...])

    for c in range(tc // HG_CHUNK):
        chunk_body(c)


def _hgrn_layer(x, mix_w, w_in, lb_logits, norm_w, w_out):
    bsz, seq, d = x.shape
    dv = norm_w.shape[0]
    nh = w_out.shape[0] // dv
    dk = (w_in.shape[1] - 2 * nh * dv) // (2 * nh)
    assert lb_logits.shape[0] == 2 and dk == LANES and dv == LANES
    tc = HG_TOKENS
    assert seq % tc == 0 and tc % HG_CHUNK == 0
    const = lambda b, j: (0, 0)
    return pl.pallas_call(
        functools.partial(_hgrn_kernel, nh=nh, dk=dk, dv=dv),
        grid=(bsz, seq // tc),
        in_specs=[
            pl.BlockSpec((1, tc, d), lambda b, j: (b, j, 0)),
            pl.BlockSpec((1, d), const),
            pl.BlockSpec(w_in.shape, const, pipeline_mode=pl.Buffered(1)),
            pl.BlockSpec(lb_logits.shape, const),
            pl.BlockSpec((1, nh * dv), const),
            pl.BlockSpec(w_out.shape, const, pipeline_mode=pl.Buffered(1)),
        ],
        out_specs=pl.BlockSpec((1, tc, d), lambda b, j: (b, j, 0)),
        out_shape=jax.ShapeDtypeStruct(x.shape, F32),
        scratch_shapes=[
            pltpu.VMEM((tc, w_in.shape[1]), F32),
            pltpu.VMEM((nh, dv, dk), F32),
        ],
        compiler_params=pltpu.CompilerParams(
            dimension_semantics=("arbitrary", "arbitrary"), vmem_limit_bytes=VMEM_LIMIT),
        name="hgrn_mixer",
    )(x, mix_w.reshape(1, d), w_in.astype(BF16), lb_logits, jnp.tile(norm_w, nh).reshape(1, nh * dv),
      w_out.astype(BF16))


def _swiglu_rows(h, wg_ref, wu_ref, wd_ref, acc):
    dff = wg_ref.shape[-1]
    for c0 in range(0, dff, FFN_CHUNK):
        g = _dot(h, wg_ref[:, c0:c0 + FFN_CHUNK])
        u = _dot(h, wu_ref[:, c0:c0 + FFN_CHUNK])
        a = (g * _sigmoid(g) * u).astype(BF16)
        y = _dot(a, wd_ref[c0:c0 + FFN_CHUNK, :])
        acc = y if acc is None else acc + y
    return acc


def _ffn_kernel(x_ref, nw_ref, wg_ref, wu_ref, wd_ref, o_ref):
    x = x_ref[...]
    h = _rms_rows(x, nw_ref[...]).astype(BF16)
    o_ref[...] = _swiglu_rows(h, wg_ref, wu_ref, wd_ref, x)


def _ffn_layer(x2, norm_w, w_gate, w_up, w_down):
    t, d = x2.shape
    dff = w_gate.shape[1]
    assert t % FFN_TOKENS == 0 and dff % FFN_CHUNK == 0
    const = lambda i: (0, 0)
    return pl.pallas_call(
        _ffn_kernel,
        grid=(t // FFN_TOKENS,),
        in_specs=[
            pl.BlockSpec((FFN_TOKENS, d), lambda i: (i, 0)),
            pl.BlockSpec((1, d), const),
            pl.BlockSpec((d, dff), const, pipeline_mode=pl.Buffered(1)),
            pl.BlockSpec((d, dff), const, pipeline_mode=pl.Buffered(1)),
            pl.BlockSpec((dff, d), const, pipeline_mode=pl.Buffered(1)),
        ],
        out_specs=pl.BlockSpec((FFN_TOKENS, d), lambda i: (i, 0)),
        out_shape=jax.ShapeDtypeStruct(x2.shape, F32),
        compiler_params=pltpu.CompilerParams(
            dimension_semantics=("arbitrary",), vmem_limit_bytes=VMEM_LIMIT),
        name="dense_ffn",
    )(x2, norm_w.reshape(1, d), w_gate.astype(BF16), w_up.astype(BF16), w_down.astype(BF16))


def _ssd_mix_block(proj_s, r0, side, x_ref, cw_ref, cb_ref, dtb_ref, alog_ref, dsk_ref, nw_ref, wout_ref, o_ref,
                   cbuf, y_s, st_s, *, d_inner):
    tc = SSD_BLOCK
    p2 = 2 * SSD_HEADDIM
    assert p2 == LANES and SSD_STATE == LANES
    gs = SSD_GROUPS * SSD_STATE
    cd = d_inner + 2 * gs
    pairs_per_group = d_inner // (SSD_GROUPS * p2)
    q = SSD_CHUNK
    assert q == LANES
    n_points = cd // LANES + (tc // q) * SSD_GROUPS * pairs_per_group
    emitted = [0, 0]

    def interleave():
        emitted[0] += 1
        while emitted[1] < len(side) and emitted[1] * n_points < emitted[0] * len(side):
            side[emitted[1]]()
            emitted[1] += 1

    for j in range(cd // LANES):
        c0 = d_inner + j * LANES
        cbuf[j, CONV_TAIL:CONV_TAIL + tc, :] = proj_s[:, c0:c0 + LANES]
        conv = cb_ref[:, j * LANES:(j + 1) * LANES]
        for k in range(CONV_W):
            off = CONV_TAIL - (CONV_W - 1) + k
            conv = conv + cw_ref[k:k + 1, j * LANES:(j + 1) * LANES] * cbuf[j, off:off + tc, :]
        cbuf[j, 0:CONV_TAIL, :] = cbuf[j, tc:tc + CONV_TAIL, :]
        proj_s[:, c0:c0 + LANES] = conv * _sigmoid(conv)
        interleave()

    dt_pre = proj_s[:, d_inner + cd:d_inner + cd + LANES] + dtb_ref[...]
    dt = jnp.maximum(dt_pre, 0.0) + jnp.log1p(jnp.exp(-jnp.abs(dt_pre)))
    da = dt * (-LOG2_E * jnp.exp(alog_ref[...]))

    tri = _tri_ones(q)
    rr = lax.broadcasted_iota(jnp.int32, (q, q), 0)
    cc = lax.broadcasted_iota(jnp.int32, (q, q), 1)
    causal = rr >= cc
    first = lax.broadcasted_iota(jnp.int32, (1, LANES), 1) < SSD_HEADDIM

    def lane_col(a, hd):
        return jnp.broadcast_to(a[:, hd:hd + 1], (q, LANES))

    for sc in range(tc // q):
        lo, hi = sc * q, (sc + 1) * q
        dtc = dt[lo:hi]
        cs = _cumsum_rows(tri, da[lo:hi])
        last = cs[q - 1:q, :]
        cs_t = cs.T
        for g in range(SSD_GROUPS):
            bm = proj_s[lo:hi, 2 * d_inner + g * SSD_STATE:2 * d_inner + (g + 1) * SSD_STATE]
            cm = proj_s[lo:hi, 2 * d_inner + gs + g * SSD_STATE:2 * d_inner + gs + (g + 1) * SSD_STATE]
            bmb = bm.astype(BF16)
            cmb = cm.astype(BF16)
            cb = _dot_nt(cmb, bmb)
            for pr in range(pairs_per_group):
                pi = g * pairs_per_group + pr
                h1, h2 = 2 * pi, 2 * pi + 1
                xp = proj_s[lo:hi, d_inner + pi * p2:d_inner + (pi + 1) * p2]
                c1 = lane_col(cs, h1)
                c2 = lane_col(cs, h2)
                cs_p = jnp.where(first, c1, c2)
                last_p = jnp.where(first, last[:, h1:h1 + 1], last[:, h2:h2 + 1])
                xdt = jnp.where(first, lane_col(dtc, h1), lane_col(dtc, h2)) * xp
                m1 = cb * jnp.exp2(jnp.where(causal, c1 - cs_t[h1:h1 + 1, :], -jnp.inf))
                m2 = cb * jnp.exp2(jnp.where(causal, c2 - cs_t[h2:h2 + 1, :], -jnp.inf))
                x1 = jnp.where(first, xdt, 0.0).astype(BF16)
                x2 = jnp.where(first, 0.0, xdt).astype(BF16)
                y = _dot(m1.astype(BF16), x1) + _dot(m2.astype(BF16), x2)
                st = st_s[pi]
                y = y + jnp.exp2(cs_p) * _dot(cmb, st.astype(BF16))
                wx = (jnp.exp2(last_p - cs_p) * xdt).astype(BF16)
                st_s[pi] = st * jnp.exp2(last_p) + _dot_tn(bmb, wx)
                y_s[lo:hi, pi * p2:(pi + 1) * p2] = y + dsk_ref[:, pi * p2:(pi + 1) * p2] * xp
                interleave()

    z = proj_s[:, 0:d_inner]
    y = y_s[...] * (z * _sigmoid(z))
    gw = d_inner // SSD_GROUPS
    parts = []
    for g in range(SSD_GROUPS):
        yg = y[:, g * gw:(g + 1) * gw]
        parts.append(yg * lax.rsqrt(jnp.mean(yg * yg, axis=-1, keepdims=True) + EPS))
    yn = jnp.concatenate(parts, axis=1) * nw_ref[...]
    o_ref[r0:r0 + tc, :] = x_ref[r0:r0 + tc, :] + _dot(yn.astype(BF16), wout_ref[...])


def _ssd_kernel(x_ref, xn_ref, mw_ref, win_ref, cw_ref, cb_ref, dtb_ref, alog_ref, dsk_ref, nw_ref, wout_ref, o_ref,
                proj_a, proj_b, cbuf, y_s, st_s, *, d_inner, steps_per_seq):
    tb = SSD_BLOCK
    n_all = win_ref.shape[1]
    step = pl.program_id(0)

    @pl.when(lax.rem(step, steps_per_seq) == 0)
    def _():
        st_s[...] = jnp.zeros_like(st_s)
        cbuf[:, 0:CONV_TAIL, :] = jnp.zeros((cbuf.shape[0], CONV_TAIL, LANES), F32)

    def projection_pieces(rows_ref, r0, dst):
        cache = []

        def piece(c0, c1):
            if not cache:
                cache.append(_rms_rows(rows_ref[r0:r0 + tb, :], mw_ref[...]).astype(BF16))
            dst[:, c0:c1] = _dot(cache[0], win_ref[:, c0:c1])

        return [functools.partial(piece, c0, min(c0 + SSD_PROJ_SLAB, n_all)) for c0 in range(0, n_all, SSD_PROJ_SLAB)]

    @pl.when(step == 0)
    def _():
        for piece in projection_pieces(x_ref, 0, proj_a):
            piece()

    mix = functools.partial(_ssd_mix_block, x_ref=x_ref, cw_ref=cw_ref, cb_ref=cb_ref, dtb_ref=dtb_ref,
                            alog_ref=alog_ref, dsk_ref=dsk_ref, nw_ref=nw_ref, wout_ref=wout_ref, o_ref=o_ref,
                            cbuf=cbuf, y_s=y_s, st_s=st_s, d_inner=d_inner)
    mix(proj_a, 0, projection_pieces(x_ref, tb, proj_b))
    mix(proj_b, tb, projection_pieces(xn_ref, 0, proj_a))


def _ssd_layer(x, mix_w, w_in, conv_w, conv_b, dt_bias, a_log, d_skip, norm_w, w_out):
    bsz, seq, d = x.shape
    d_inner = w_out.shape[0]
    nheads = dt_bias.shape[0]
    assert d_inner == nheads * SSD_HEADDIM and nheads <= LANES
    cd = d_inner + 2 * SSD_GROUPS * SSD_STATE
    assert conv_w.shape == (CONV_W, cd) and w_in.shape[1] == d_inner + cd + nheads
    tb = SSD_BLOCK
    assert seq % (2 * tb) == 0 and tb % SSD_CHUNK == 0
    t = bsz * seq
    pad = LANES - nheads
    w_all = jnp.pad(w_in, ((0, 0), (0, pad))).astype(BF16)
    n_all = w_all.shape[1]
    const = lambda i: (0, 0)
    last_block = t // tb - 1
    x2 = x.reshape(t, d)
    out = pl.pallas_call(
        functools.partial(_ssd_kernel, d_inner=d_inner, steps_per_seq=seq // (2 * tb)),
        grid=(t // (2 * tb),),
        in_specs=[
            pl.BlockSpec((2 * tb, d), lambda i: (i, 0)),
            pl.BlockSpec((tb, d), lambda i: (jnp.minimum(2 * i + 2, last_block), 0)),
            pl.BlockSpec((1, d), const),
            pl.BlockSpec(w_all.shape, const, pipeline_mode=pl.Buffered(1)),
            pl.BlockSpec((CONV_W, cd), const),
            pl.BlockSpec((1, cd), const),
            pl.BlockSpec((1, LANES), const),
            pl.BlockSpec((1, LANES), const),
            pl.BlockSpec((1, d_inner), const),
            pl.BlockSpec((1, d_inner), const),
            pl.BlockSpec(w_out.shape, const, pipeline_mode=pl.Buffered(1)),
        ],
        out_specs=pl.BlockSpec((2 * tb, d), lambda i: (i, 0)),
        out_shape=jax.ShapeDtypeStruct((t, d), F32),
        scratch_shapes=[
            pltpu.VMEM((tb, n_all), F32),
            pltpu.VMEM((tb, n_all), F32),
            pltpu.VMEM((cd // LANES, tb + CONV_TAIL, LANES), F32),
            pltpu.VMEM((tb, d_inner), F32),
            pltpu.VMEM((nheads // 2, SSD_STATE, LANES), F32),
        ],
        compiler_params=pltpu.CompilerParams(
            dimension_semantics=("arbitrary",), vmem_limit_bytes=VMEM_LIMIT),
        name="ssd_mixer",
    )(x2, x2, mix_w.reshape(1, d), w_all, conv_w, conv_b.reshape(1, cd),
      jnp.pad(dt_bias, (0, pad)).reshape(1, LANES), jnp.pad(a_log, (0, pad)).reshape(1, LANES),
      jnp.repeat(d_skip, SSD_HEADDIM).reshape(1, d_inner), norm_w.reshape(1, d_inner), w_out.astype(BF16))
    return out.reshape(bsz, seq, d)


def _router_kernel(x_ref, nw_ref, rw_ref, idx_ref, gate_ref):
    h = _rms_rows(x_ref[...], nw_ref[...])
    hi = h.astype(BF16)
    lo = (h - hi.astype(F32)).astype(BF16)
    both = _dot(hi, rw_ref[...])
    logits = both[:, :LANES] + both[:, LANES:] + _dot(lo, rw_ref[:, :LANES])
    lane = lax.broadcasted_iota(jnp.int32, logits.shape, 1)
    valid = lane < MOE_EXPERTS
    lg = jnp.where(valid, logits, -jnp.inf)
    e = jnp.exp(lg - jnp.max(lg, axis=-1, keepdims=True))
    p = jnp.where(valid, e / jnp.sum(e, axis=-1, keepdims=True), -1.0)
    m1 = jnp.max(p, axis=-1, keepdims=True)
    i1 = jnp.min(jnp.where(p == m1, lane, LANES), axis=-1, keepdims=True)
    p2 = jnp.where(lane == i1, -1.0, p)
    m2 = jnp.max(p2, axis=-1, keepdims=True)
    i2 = jnp.min(jnp.where(p2 == m2, lane, LANES), axis=-1, keepdims=True)
    s = m1 + m2
    idx_ref[...] = jnp.where(lane == 0, i1, jnp.where(lane == 1, i2, 0))
    gate_ref[...] = jnp.where(lane == 0, m1 / s, jnp.where(lane == 1, m2 / s, 0.0))


def _router(x2, norm_w, router_w):
    t, d = x2.shape
    ne = router_w.shape[1]
    assert ne == MOE_EXPERTS and t % ROUTER_TOKENS == 0
    rw = jnp.pad(router_w, ((0, 0), (0, LANES - ne)))
    rw_hi = rw.astype(BF16)
    rw_lo = (rw - rw_hi.astype(F32)).astype(BF16)
    rw = jnp.concatenate([rw_hi, rw_lo], axis=1)
    return pl.pallas_call(
        _router_kernel,
        grid=(t // ROUTER_TOKENS,),
        in_specs=[
            pl.BlockSpec((ROUTER_TOKENS, d), lambda i: (i, 0)),
            pl.BlockSpec((1, d), lambda i: (0, 0)),
            pl.BlockSpec((d, 2 * LANES), lambda i: (0, 0)),
        ],
        out_specs=[pl.BlockSpec((ROUTER_TOKENS, LANES), lambda i: (i, 0))] * 2,
        out_shape=[jax.ShapeDtypeStruct((t, LANES), jnp.int32), jax.ShapeDtypeStruct((t, LANES), F32)],
        compiler_params=pltpu.CompilerParams(dimension_semantics=("arbitrary",)),
        name="moe_router",
    )(x2, norm_w.reshape(1, d), rw)


def _row_gather_start(src_hbm, dst, row_of, n, sem):
    for r in range(n):
        pltpu.make_async_copy(src_hbm.at[pl.ds(row_of(r), 1), :], dst.at[pl.ds(r, 1), :], sem).start()


def _row_gather_wait(src_hbm, dst, n, sem):
    pltpu.make_async_copy(src_hbm.at[pl.ds(0, n), :], dst, sem).wait()


def _dispatch_kernel(pad_ref, pos_ref, x_ref, xs_hbm, xbuf, zbuf, sem, pad_sem):
    i = pl.program_id(0)
    last = pl.num_programs(0) - 1
    tb = x_ref.shape[0]
    slot = lax.rem(i, 2)

    xbuf[slot] = x_ref[...]

    def issue(r, carry):
        src = xbuf.at[slot, pl.ds(r, 1), :]
        pltpu.make_async_copy(src, xs_hbm.at[pl.ds(pos_ref[0, 0, 2 * r], 1), :], sem.at[slot]).start()
        pltpu.make_async_copy(src, xs_hbm.at[pl.ds(pos_ref[0, 0, 2 * r + 1], 1), :], sem.at[slot]).start()
        return carry

    for r in range(tb):
        issue(r, 0)

    def wait_slot(s):
        for _ in range(2):
            pltpu.make_async_copy(xbuf.at[s], xs_hbm.at[pl.ds(0, tb), :], sem.at[s]).wait()

    @pl.when(i > 0)
    def _():
        wait_slot(1 - slot)

    @pl.when(i == last)
    def _():
        wait_slot(slot)
        zbuf[...] = jnp.zeros_like(zbuf)
        for e in range(MOE_EXPERTS):
            first = pad_ref[e]
            count = pad_ref[MOE_EXPERTS + e]

            def zero_issue(r, carry, first=first):
                pltpu.make_async_copy(zbuf.at[pl.ds(0, 1), :], xs_hbm.at[pl.ds(first + r, 1), :], pad_sem).start()
                return carry

            def zero_wait(r, carry, first=first):
                pltpu.make_async_copy(zbuf.at[pl.ds(0, 1), :], xs_hbm.at[pl.ds(first, 1), :], pad_sem).wait()
                return carry

            lax.fori_loop(0, count, zero_issue, 0)
            lax.fori_loop(0, count, zero_wait, 0)

        tm = zbuf.shape[0]
        first_tile = pad_ref[2 * MOE_EXPERTS]

        def tile_copy(j):
            return pltpu.make_async_copy(zbuf, xs_hbm.at[pl.ds((first_tile + j) * tm, tm), :], pad_sem)

        def tile_issue(j, carry):
            tile_copy(j).start()
            return carry

        def tile_wait(j, carry):
            tile_copy(j).wait()
            return carry

        n_unused = xs_hbm.shape[0] // tm - first_tile
        lax.fori_loop(0, n_unused, tile_issue, 0)
        lax.fori_loop(0, n_unused, tile_wait, 0)


def _dispatch(x2, pos, pad_info, n_rows):
    t, d = x2.shape
    tb = COMBINE_TOKENS
    assert t % tb == 0
    grid_spec = pltpu.PrefetchScalarGridSpec(
        num_scalar_prefetch=1,
        grid=(t // tb,),
        in_specs=[
            pl.BlockSpec((1, 1, 2 * tb), lambda i, pad: (i, 0, 0), memory_space=pltpu.SMEM),
            pl.BlockSpec((tb, d), lambda i, pad: (i, 0)),
        ],
        out_specs=pl.BlockSpec(memory_space=pl.ANY),
        scratch_shapes=[pltpu.VMEM((2, tb, d), F32), pltpu.VMEM((MOE_TILE, d), F32),
                        pltpu.SemaphoreType.DMA((2,)), pltpu.SemaphoreType.DMA(())],
    )
    return pl.pallas_call(
        _dispatch_kernel,
        grid_spec=grid_spec,
        out_shape=jax.ShapeDtypeStruct((n_rows, d), F32),
        compiler_params=pltpu.CompilerParams(dimension_semantics=("arbitrary",)),
        name="moe_dispatch",
    )(pad_info, pos.reshape(t // tb, 1, 2 * tb), x2)


def _experts_kernel(te_ref, tv_ref, ts_ref, xs_ref, nw_ref, wg_ref, wu_ref, wd_ref, y_ref):
    i = pl.program_id(0)

    @pl.when(tv_ref[i] > 0)
    def _():
        h = _rms_rows(xs_ref[...], nw_ref[...]).astype(BF16)
        y_ref[...] = _swiglu_rows(h, wg_ref.at[0], wu_ref.at[0], wd_ref.at[0], None)

    @pl.when(tv_ref[i] == 0)
    def _():
        y_ref[...] = jnp.zeros_like(y_ref)


def _experts(xs, norm_w, tile_expert, tile_valid, tile_src, w_gate, w_up, w_down):
    n_rows, d = xs.shape
    ne, _, dff = w_gate.shape
    tm = MOE_TILE
    nt = n_rows // tm
    wmap = lambda i, te, tv, ts: (te[i], 0, 0)
    grid_spec = pltpu.PrefetchScalarGridSpec(
        num_scalar_prefetch=3,
        grid=(nt,),
        in_specs=[
            pl.BlockSpec((tm, d), lambda i, te, tv, ts: (ts[i], 0)),
            pl.BlockSpec((1, d), lambda i, te, tv, ts: (0, 0)),
            pl.BlockSpec((1, d, dff), wmap),
            pl.BlockSpec((1, d, dff), wmap),
            pl.BlockSpec((1, dff, d), wmap),
        ],
        out_specs=pl.BlockSpec((tm, d), lambda i, te, tv, ts: (i, 0)),
    )
    return pl.pallas_call(
        _experts_kernel,
        grid_spec=grid_spec,
        out_shape=jax.ShapeDtypeStruct((n_rows, d), F32),
        compiler_params=pltpu.CompilerParams(
            dimension_semantics=("arbitrary",), vmem_limit_bytes=VMEM_LIMIT),
        name="moe_experts",
    )(tile_expert, tile_valid, tile_src, xs, norm_w.reshape(1, d),
      w_gate.astype(BF16), w_up.astype(BF16), w_down.astype(BF16))


def _combine_kernel(pos_ref, posn_ref, x_ref, gate_ref, fw_ref, y_hbm, o_ref, ybuf, sem):
    i = pl.program_id(0)
    tb = x_ref.shape[0]
    slot = lax.rem(i, 2)

    def start(p_ref, s):
        for j in range(2):
            _row_gather_start(y_hbm, ybuf.at[s, j], lambda r, j=j: p_ref[0, 0, 2 * r + j], tb, sem.at[s, j])

    @pl.when(i == 0)
    def _():
        start(pos_ref, 0)

    @pl.when(i + 1 < pl.num_programs(0))
    def _():
        start(posn_ref, 1 - slot)

    for j in range(2):
        _row_gather_wait(y_hbm, ybuf.at[slot, j], tb, sem.at[slot, j])
    g = gate_ref[...]
    o = x_ref[...] + g[:, 0:1] * ybuf[slot, 0] + g[:, 1:2] * ybuf[slot, 1]
    o_ref[...] = _rms_rows(o, fw_ref[...])


def _combine(x2, final_w, pos, gate_pad, y_rows):
    t, d = x2.shape
    tb = COMBINE_TOKENS
    nb = t // tb
    pos3 = pos.reshape(nb, 1, 2 * tb)
    return pl.pallas_call(
        _combine_kernel,
        grid=(nb,),
        in_specs=[
            pl.BlockSpec((1, 1, 2 * tb), lambda i: (i, 0, 0), memory_space=pltpu.SMEM),
            pl.BlockSpec((1, 1, 2 * tb), lambda i: (jnp.minimum(i + 1, nb - 1), 0, 0), memory_space=pltpu.SMEM),
            pl.BlockSpec((tb, d), lambda i: (i, 0)),
            pl.BlockSpec((tb, LANES), lambda i: (i, 0)),
            pl.BlockSpec((1, d), lambda i: (0, 0)),
            pl.BlockSpec(memory_space=pl.ANY),
        ],
        out_specs=pl.BlockSpec((tb, d), lambda i: (i, 0)),
        out_shape=jax.ShapeDtypeStruct(x2.shape, F32),
        scratch_shapes=[pltpu.VMEM((2, 2, tb, d), F32), pltpu.SemaphoreType.DMA((2, 2))],
        compiler_params=pltpu.CompilerParams(dimension_semantics=("arbitrary",)),
        name="moe_combine",
    )(pos3, pos3, x2, gate_pad, final_w.reshape(1, d), y_rows)


def _moe_layer_and_final_norm(x2, norm_w, final_w, router_w, w_gate, w_up, w_down):
    t, d = x2.shape
    ne = MOE_EXPERTS
    tm = MOE_TILE
    idx_pad, gate_pad = _router(x2, norm_w, router_w)
    flat_e = idx_pad[:, :2].reshape(-1)
    onehot = (flat_e[:, None] == jnp.arange(ne, dtype=jnp.int32)[None, :]).astype(jnp.int32)
    csum = jnp.cumsum(onehot, axis=0)
    rank = jnp.sum(csum * onehot, axis=1) - 1
    counts = csum[-1]
    padded = ((counts + tm - 1) // tm) * tm
    ends = jnp.cumsum(padded)
    starts = ends - padded
    pos = (jnp.sum(starts[None, :] * onehot, axis=1) + rank).astype(jnp.int32)
    nt = (2 * t) // tm + ne
    tile_ids = jnp.arange(nt, dtype=jnp.int32)
    tile_valid = (tile_ids * tm < ends[-1]).astype(jnp.int32)
    tile_expert = jnp.minimum(jnp.sum((tile_ids[:, None] * tm >= ends[None, :]).astype(jnp.int32), axis=1), ne - 1)
    tile_src = tile_ids * tile_valid
    pad_info = jnp.concatenate([starts + counts, padded - counts, ends[-1:] // tm]).astype(jnp.int32)
    xs = _dispatch(x2, pos, pad_info, nt * tm)
    y_rows = _experts(xs, norm_w, tile_expert.astype(jnp.int32), tile_valid, tile_src, w_gate, w_up, w_down)
    return _combine(x2, final_w, pos, gate_pad, y_rows)


def kernel(x, mix_norm_w, ffn_norm_w, final_norm_w, hg_w_in, hg_lb_logits, hg_norm_w, hg_w_out, ssd_w_in, ssd_conv_w, ssd_conv_b, ssd_dt_bias, ssd_a_log, ssd_d, ssd_norm_w, ssd_w_out, ffn_w_gate, ffn_w_up, ffn_w_down, moe_router, moe_w_gate, moe_w_up, moe_w_down):
    bsz, seq, d = x.shape
    assert mix_norm_w.shape[0] == 2 and hg_w_in.shape[0] == 1 and ssd_w_in.shape[0] == 1
    x = _hgrn_layer(x, mix_norm_w[0], hg_w_in[0], hg_lb_logits, hg_norm_w[0], hg_w_out[0])
    x2 = _ffn_layer(x.reshape(bsz * seq, d), ffn_norm_w[0], ffn_w_gate[0], ffn_w_up[0], ffn_w_down[0])
    x = _ssd_layer(x2.reshape(bsz, seq, d), mix_norm_w[1], ssd_w_in[0], ssd_conv_w[0], ssd_conv_b[0],
                   ssd_dt_bias[0], ssd_a_log[0], ssd_d[0], ssd_norm_w[0], ssd_w_out[0])
    out = _moe_layer_and_final_norm(x.reshape(bsz * seq, d), ffn_norm_w[1], final_norm_w, moe_router[0],
                                    moe_w_gate[0], moe_w_up[0], moe_w_down[0])
    return out.reshape(bsz, seq, d)
```

```python
import functools

import jax
import jax.numpy as jnp
from jax import lax
from jax.experimental import pallas as pl
from jax.experimental.pallas import tpu as pltpu

F32 = jnp.float32
BF16 = jnp.bfloat16
EPS = 1e-6
LOG2_E = 1.4426950408889634
LANES = 128
VMEM_LIMIT = 60000 * 1024

HG_CHUNK = 256
HG_LEVELS = (128, 64, 32, 16, 8, 4, 2, 1)
HG_TOKENS = 256
SSD_CHUNK = 128
SSD_BLOCK = 256
SSD_PROJ_SLAB = 256
SSD_HEADDIM = 64
SSD_STATE = 128
SSD_GROUPS = 4
CONV_W = 4
CONV_TAIL = 8
FFN_TOKENS = 1024
FFN_CHUNK = 2816
MOE_EXPERTS = 8
MOE_TILE = 512
ROUTER_TOKENS = 512
COMBINE_TOKENS = 256
MOE_RUN_ALIGN = 8
MOE_RUN_PIECES = tuple(1 << k for k in range(9, 2, -1))
MOE_STAGE_ROWS = 640


def _dot(a, b):
    return jnp.dot(a, b, preferred_element_type=F32)


def _dot_nt(a, b):
    return lax.dot_general(a, b, (((1,), (1,)), ((), ())), preferred_element_type=F32)


def _dot_tn(a, b):
    return lax.dot_general(a, b, (((0,), (0,)), ((), ())), preferred_element_type=F32)


def _sigmoid(x):
    return 0.5 * jnp.tanh(0.5 * x) + 0.5


def _rms_rows(x, w):
    return x * lax.rsqrt(jnp.mean(x * x, axis=-1, keepdims=True) + EPS) * w


def _tri_ones(n):
    r = lax.broadcasted_iota(jnp.int32, (n, n), 0)
    c = lax.broadcasted_iota(jnp.int32, (n, n), 1)
    return jnp.where(r >= c, 1.0, 0.0).astype(BF16)


def _cumsum_rows(tri, x):
    hi = x.astype(BF16)
    r1 = x - hi.astype(F32)
    mid = r1.astype(BF16)
    lo = (r1 - mid.astype(F32)).astype(BF16)
    return _dot(tri, hi) + _dot(tri, mid) + _dot(tri, lo)


def _hgrn_kernel(x_ref, mw_ref, win_ref, lbl_ref, nw_ref, wout_ref, o_ref, proj_s, st_s, *, nh, dk, dv):
    tc = x_ref.shape[1]
    qk = nh * dk
    vd = nh * dv

    @pl.when(pl.program_id(1) == 0)
    def _():
        st_s[...] = jnp.zeros_like(st_s)

    x = x_ref[0]
    h = _rms_rows(x, mw_ref[...]).astype(BF16)
    for r0 in range(0, tc, HG_CHUNK):
        proj_s[r0:r0 + HG_CHUNK, :] = _dot(h[r0:r0 + HG_CHUNK], win_ref[...])

    lbl = lbl_ref[...]
    lmax = jnp.max(lbl, axis=0, keepdims=True)
    le = jnp.exp(lbl - lmax)
    lb = le[0:1, :] / jnp.sum(le, axis=0, keepdims=True)

    tri = _tri_ones(HG_CHUNK)
    half = HG_CHUNK // 2
    assert HG_LEVELS[0] == half
    ti = lax.broadcasted_iota(jnp.int32, (half, half), 0)
    si = lax.broadcasted_iota(jnp.int32, (half, half), 1)
    level = jnp.where(si > ti, -2, 31 - lax.clz(ti ^ si))
    level_ids = [hs.bit_length() - 1 for hs in HG_LEVELS[1:]] + [-1]
    sub8 = lax.broadcasted_iota(jnp.int32, (HG_CHUNK // 8, 8, qk), 1)

    def level_decay(b, hs):
        if hs >= 8:
            parts = []
            for lo in range(0, HG_CHUNK, 2 * hs):
                r = b[lo + hs - 1:lo + hs, :]
                parts += [r - b[lo:lo + hs], b[lo + hs:lo + 2 * hs] - r]
            return jnp.exp2(jnp.concatenate(parts, axis=0)).astype(BF16)
        b3 = b.reshape(HG_CHUNK // 8, 8, qk)
        ref = jnp.broadcast_to(b3[:, hs - 1:hs, :], b3.shape)
        for lo in range(2 * hs, 8, 2 * hs):
            ref = jnp.where(sub8 >= lo, jnp.broadcast_to(b3[:, lo + hs - 1:lo + hs, :], b3.shape), ref)
        return jnp.exp2(-jnp.abs(b - ref.reshape(HG_CHUNK, qk))).astype(BF16)

    def chunk_body(c):
        r0 = c * HG_CHUNK
        q = proj_s[pl.ds(r0, HG_CHUNK), 0:qk]
        fp = proj_s[pl.ds(r0, HG_CHUNK), qk:2 * qk]
        v = proj_s[pl.ds(r0, HG_CHUNK), 2 * qk:2 * qk + vd]
        f = lb + (1.0 - lb) * _sigmoid(fp)
        k = 1.0 - f
        b = _cumsum_rows(tri, jnp.log2(f))
        b_last = b[HG_CHUNK - 1:HG_CHUNK, :]
        qg = (q * jnp.exp2(b)).astype(BF16)
        kdec = (k * jnp.exp2(b_last - b)).astype(BF16)
        g_last = jnp.exp2(b_last)
        vb = v.astype(BF16)

        qb = q.astype(BF16)
        kb = k.astype(BF16)
        z_top = level_decay(b, half)
        qd_top = qb[half:] * z_top[half:]
        kd_top = kb[:half] * z_top[:half]
        qds = []
        kds = []
        for hs in HG_LEVELS[1:]:
            z = level_decay(b, hs)
            qds.append(qb * z)
            kds.append(kb * z)
        qds.append(qb)
        kds.append(kb)

        def diagonal_block(rows, ks):
            attn = jnp.zeros((half, half), F32)
            for qd, kd, lid in zip(qds, kds, level_ids):
                attn = jnp.where(level == lid, _dot_nt(qd[rows, ks], kd[rows, ks]), attn)
            return attn

        o_heads = []
        for hh in range(nh):
            ks = slice(hh * dk, (hh + 1) * dk)
            vh = vb[:, hh * dv:(hh + 1) * dv]
            attn_lo = diagonal_block(slice(0, half), ks)
            attn_hi = jnp.concatenate([_dot_nt(qd_top[:, ks], kd_top[:, ks]), diagonal_block(slice(half, HG_CHUNK), ks)],
                                      axis=1)
            o_intra = jnp.concatenate([_dot(attn_lo.astype(BF16), vh[:half]), _dot(attn_hi.astype(BF16), vh)], axis=0)
            st = st_s[hh]
            o_inter = _dot_nt(qg[:, ks], st.astype(BF16))
            oh = o_inter + o_intra
            o_heads.append(oh * lax.rsqrt(jnp.mean(oh * oh, axis=-1, keepdims=True) + EPS))
            st_s[hh] = st * g_last[:, ks] + _dot_tn(vb[:, hh * dv:(hh + 1) * dv], kdec[:, ks])
        g = proj_s[pl.ds(r0, HG_CHUNK), 2 * qk + vd:2 * qk + 2 * vd]
        on = jnp.concatenate(o_heads, axis=1) * nw_ref[...] * _sigmoid(g)
        o_ref[0, pl.ds(r0, HG_CHUNK), :] = x[r0:r0 + HG_CHUNK] + _dot(on.astype(BF16), wout_ref[...])

    for c in range(tc // HG_CHUNK):
        chunk_body(c)


def _hgrn_layer(x, mix_w, w_in, lb_logits, norm_w, w_out):
    bsz, seq, d = x.shape
    dv = norm_w.shape[0]
    nh = w_out.shape[0] // dv
    dk = (w_in.shape[1] - 2 * nh * dv) // (2 * nh)
    assert lb_logits.shape[0] == 2 and dk == LANES and dv == LANES
    tc = HG_TOKENS
    assert seq % tc == 0 and tc % HG_CHUNK == 0
    const = lambda b, j: (0, 0)
    return pl.pallas_call(
        functools.partial(_hgrn_kernel, nh=nh, dk=dk, dv=dv),
        grid=(bsz, seq // tc),
        in_specs=[
            pl.BlockSpec((1, tc, d), lambda b, j: (b, j, 0)),
            pl.BlockSpec((1, d), const),
            pl.BlockSpec(w_in.shape, const, pipeline_mode=pl.Buffered(1)),
            pl.BlockSpec(lb_logits.shape, const),
            pl.BlockSpec((1, nh * dv), const),
            pl.BlockSpec(w_out.shape, const, pipeline_mode=pl.Buffered(1)),
        ],
        out_specs=pl.BlockSpec((1, tc, d), lambda b, j: (b, j, 0)),
        out_shape=jax.ShapeDtypeStruct(x.shape, F32),
        scratch_shapes=[
            pltpu.VMEM((tc, w_in.shape[1]), F32),
            pltpu.VMEM((nh, dv, dk), F32),
        ],
        compiler_params=pltpu.CompilerParams(
            dimension_semantics=("arbitrary", "arbitrary"), vmem_limit_bytes=VMEM_LIMIT),
        name="hgrn_mixer",
    )(x, mix_w.reshape(1, d), w_in.astype(BF16), lb_logits, jnp.tile(norm_w, nh).reshape(1, nh * dv),
      w_out.astype(BF16))


def _swiglu_rows(h, wg_ref, wu_ref, wd_ref, acc):
    dff = wg_ref.shape[-1]
    for c0 in range(0, dff, FFN_CHUNK):
        g = _dot(h, wg_ref[:, c0:c0 + FFN_CHUNK])
        u = _dot(h, wu_ref[:, c0:c0 + FFN_CHUNK])
        a = (g * _sigmoid(g) * u).astype(BF16)
        y = _dot(a, wd_ref[c0:c0 + FFN_CHUNK, :])
        acc = y if acc is None else acc + y
    return acc


def _ffn_kernel(x_ref, nw_ref, wg_ref, wu_ref, wd_ref, o_ref):
    x = x_ref[...]
    h = _rms_rows(x, nw_ref[...]).astype(BF16)
    o_ref[...] = _swiglu_rows(h, wg_ref, wu_ref, wd_ref, x)


def _ffn_layer(x2, norm_w, w_gate, w_up, w_down):
    t, d = x2.shape
    dff = w_gate.shape[1]
    assert t % FFN_TOKENS == 0 and dff % FFN_CHUNK == 0
    const = lambda i: (0, 0)
    return pl.pallas_call(
        _ffn_kernel,
        grid=(t // FFN_TOKENS,),
        in_specs=[
            pl.BlockSpec((FFN_TOKENS, d), lambda i: (i, 0)),
            pl.BlockSpec((1, d), const),
            pl.BlockSpec((d, dff), const, pipeline_mode=pl.Buffered(1)),
            pl.BlockSpec((d, dff), const, pipeline_mode=pl.Buffered(1)),
            pl.BlockSpec((dff, d), const, pipeline_mode=pl.Buffered(1)),
        ],
        out_specs=pl.BlockSpec((FFN_TOKENS, d), lambda i: (i, 0)),
        out_shape=jax.ShapeDtypeStruct(x2.shape, F32),
        compiler_params=pltpu.CompilerParams(
            dimension_semantics=("arbitrary",), vmem_limit_bytes=VMEM_LIMIT),
        name="dense_ffn",
    )(x2, norm_w.reshape(1, d), w_gate.astype(BF16), w_up.astype(BF16), w_down.astype(BF16))


def _ssd_mix_block(proj_s, r0, side, x_ref, cw_ref, cb_ref, dtb_ref, alog_ref, dsk_ref, nw_ref, wout_ref, o_ref,
                   cbuf, y_s, st_s, *, d_inner):
    tc = SSD_BLOCK
    p2 = 2 * SSD_HEADDIM
    assert p2 == LANES and SSD_STATE == LANES
    gs = SSD_GROUPS * SSD_STATE
    cd = d_inner + 2 * gs
    pairs_per_group = d_inner // (SSD_GROUPS * p2)
    q = SSD_CHUNK
    assert q == LANES
    n_points = cd // LANES + (tc // q) * SSD_GROUPS * pairs_per_group
    emitted = [0, 0]

    def interleave():
        emitted[0] += 1
        while emitted[1] < len(side) and emitted[1] * n_points < emitted[0] * len(side):
            side[emitted[1]]()
            emitted[1] += 1

    for j in range(cd // LANES):
        c0 = d_inner + j * LANES
        cbuf[j, CONV_TAIL:CONV_TAIL + tc, :] = proj_s[:, c0:c0 + LANES]
        conv = cb_ref[:, j * LANES:(j + 1) * LANES]
        for k in range(CONV_W):
            off = CONV_TAIL - (CONV_W - 1) + k
            conv = conv + cw_ref[k:k + 1, j * LANES:(j + 1) * LANES] * cbuf[j, off:off + tc, :]
        cbuf[j, 0:CONV_TAIL, :] = cbuf[j, tc:tc + CONV_TAIL, :]
        proj_s[:, c0:c0 + LANES] = conv * _sigmoid(conv)
        interleave()

    dt_pre = proj_s[:, d_inner + cd:d_inner + cd + LANES] + dtb_ref[...]
    dt = jnp.maximum(dt_pre, 0.0) + jnp.log1p(jnp.exp(-jnp.abs(dt_pre)))
    da = dt * (-LOG2_E * jnp.exp(alog_ref[...]))

    tri = _tri_ones(q)
    rr = lax.broadcasted_iota(jnp.int32, (q, q), 0)
    cc = lax.broadcasted_iota(jnp.int32, (q, q), 1)
    causal = rr >= cc
    first = lax.broadcasted_iota(jnp.int32, (1, LANES), 1) < SSD_HEADDIM

    def lane_col(a, hd):
        return jnp.broadcast_to(a[:, hd:hd + 1], (q, LANES))

    for sc in range(tc // q):
        lo, hi = sc * q, (sc + 1) * q
        dtc = dt[lo:hi]
        cs = _cumsum_rows(tri, da[lo:hi])
        last = cs[q - 1:q, :]
        cs_t = cs.T
        for g in range(SSD_GROUPS):
            bm = proj_s[lo:hi, 2 * d_inner + g * SSD_STATE:2 * d_inner + (g + 1) * SSD_STATE]
            cm = proj_s[lo:hi, 2 * d_inner + gs + g * SSD_STATE:2 * d_inner + gs + (g + 1) * SSD_STATE]
            bmb = bm.astype(BF16)
            cmb = cm.astype(BF16)
            cb = _dot_nt(cmb, bmb)
            for pr in range(pairs_per_group):
                pi = g * pairs_per_group + pr
                h1, h2 = 2 * pi, 2 * pi + 1
                xp = proj_s[lo:hi, d_inner + pi * p2:d_inner + (pi + 1) * p2]
                c1 = lane_col(cs, h1)
                c2 = lane_col(cs, h2)
                cs_p = jnp.where(first, c1, c2)
                last_p = jnp.where(first, last[:, h1:h1 + 1], last[:, h2:h2 + 1])
                xdt = jnp.where(first, lane_col(dtc, h1), lane_col(dtc, h2)) * xp
                m1 = cb * jnp.exp2(jnp.where(causal, c1 - cs_t[h1:h1 + 1, :], -jnp.inf))
                m2 = cb * jnp.exp2(jnp.where(causal, c2 - cs_t[h2:h2 + 1, :], -jnp.inf))
                x1 = jnp.where(first, xdt, 0.0).astype(BF16)
                x2 = jnp.where(first, 0.0, xdt).astype(BF16)
                y = _dot(m1.astype(BF16), x1) + _dot(m2.astype(BF16), x2)
                st = st_s[pi]
                y = y + jnp.exp2(cs_p) * _dot(cmb, st.astype(BF16))
                wx = (jnp.exp2(last_p - cs_p) * xdt).astype(BF16)
                st_s[pi] = st * jnp.exp2(last_p) + _dot_tn(bmb, wx)
                y_s[lo:hi, pi * p2:(pi + 1) * p2] = y + dsk_ref[:, pi * p2:(pi + 1) * p2] * xp
                interleave()

    z = proj_s[:, 0:d_inner]
    y = y_s[...] * (z * _sigmoid(z))
    gw = d_inner // SSD_GROUPS
    parts = []
    for g in range(SSD_GROUPS):
        yg = y[:, g * gw:(g + 1) * gw]
        parts.append(yg * lax.rsqrt(jnp.mean(yg * yg, axis=-1, keepdims=True) + EPS))
    yn = jnp.concatenate(parts, axis=1) * nw_ref[...]
    o_ref[r0:r0 + tc, :] = x_ref[r0:r0 + tc, :] + _dot(yn.astype(BF16), wout_ref[...])


def _ssd_kernel(x_ref, xn_ref, mw_ref, win_ref, cw_ref, cb_ref, dtb_ref, alog_ref, dsk_ref, nw_ref, wout_ref, o_ref,
                proj_a, proj_b, cbuf, y_s, st_s, *, d_inner, steps_per_seq):
    tb = SSD_BLOCK
    n_all = win_ref.shape[1]
    step = pl.program_id(0)

    @pl.when(lax.rem(step, steps_per_seq) == 0)
    def _():
        st_s[...] = jnp.zeros_like(st_s)
        cbuf[:, 0:CONV_TAIL, :] = jnp.zeros((cbuf.shape[0], CONV_TAIL, LANES), F32)

    def projection_pieces(rows_ref, r0, dst):
        cache = []

        def piece(c0, c1):
            if not cache:
                cache.append(_rms_rows(rows_ref[r0:r0 + tb, :], mw_ref[...]).astype(BF16))
            dst[:, c0:c1] = _dot(cache[0], win_ref[:, c0:c1])

        return [functools.partial(piece, c0, min(c0 + SSD_PROJ_SLAB, n_all)) for c0 in range(0, n_all, SSD_PROJ_SLAB)]

    @pl.when(step == 0)
    def _():
        for piece in projection_pieces(x_ref, 0, proj_a):
            piece()

    mix = functools.partial(_ssd_mix_block, x_ref=x_ref, cw_ref=cw_ref, cb_ref=cb_ref, dtb_ref=dtb_ref,
                            alog_ref=alog_ref, dsk_ref=dsk_ref, nw_ref=nw_ref, wout_ref=wout_ref, o_ref=o_ref,
                            cbuf=cbuf, y_s=y_s, st_s=st_s, d_inner=d_inner)
    mix(proj_a, 0, projection_pieces(x_ref, tb, proj_b))
    mix(proj_b, tb, projection_pieces(xn_ref, 0, proj_a))


def _ssd_layer(x, mix_w, w_in, conv_w, conv_b, dt_bias, a_log, d_skip, norm_w, w_out):
    bsz, seq, d = x.shape
    d_inner = w_out.shape[0]
    nheads = dt_bias.shape[0]
    assert d_inner == nheads * SSD_HEADDIM and nheads <= LANES
    cd = d_inner + 2 * SSD_GROUPS * SSD_STATE
    assert conv_w.shape == (CONV_W, cd) and w_in.shape[1] == d_inner + cd + nheads
    tb = SSD_BLOCK
    assert seq % (2 * tb) == 0 and tb % SSD_CHUNK == 0
    t = bsz * seq
    pad = LANES - nheads
    w_all = jnp.pad(w_in, ((0, 0), (0, pad))).astype(BF16)
    n_all = w_all.shape[1]
    const = lambda i: (0, 0)
    last_block = t // tb - 1
    x2 = x.reshape(t, d)
    out = pl.pallas_call(
        functools.partial(_ssd_kernel, d_inner=d_inner, steps_per_seq=seq // (2 * tb)),
        grid=(t // (2 * tb),),
        in_specs=[
            pl.BlockSpec((2 * tb, d), lambda i: (i, 0)),
            pl.BlockSpec((tb, d), lambda i: (jnp.minimum(2 * i + 2, last_block), 0)),
            pl.BlockSpec((1, d), const),
            pl.BlockSpec(w_all.shape, const, pipeline_mode=pl.Buffered(1)),
            pl.BlockSpec((CONV_W, cd), const),
            pl.BlockSpec((1, cd), const),
            pl.BlockSpec((1, LANES), const),
            pl.BlockSpec((1, LANES), const),
            pl.BlockSpec((1, d_inner), const),
            pl.BlockSpec((1, d_inner), const),
            pl.BlockSpec(w_out.shape, const, pipeline_mode=pl.Buffered(1)),
        ],
        out_specs=pl.BlockSpec((2 * tb, d), lambda i: (i, 0)),
        out_shape=jax.ShapeDtypeStruct((t, d), F32),
        scratch_shapes=[
            pltpu.VMEM((tb, n_all), F32),
            pltpu.VMEM((tb, n_all), F32),
            pltpu.VMEM((cd // LANES, tb + CONV_TAIL, LANES), F32),
            pltpu.VMEM((tb, d_inner), F32),
            pltpu.VMEM((nheads // 2, SSD_STATE, LANES), F32),
        ],
        compiler_params=pltpu.CompilerParams(
            dimension_semantics=("arbitrary",), vmem_limit_bytes=VMEM_LIMIT),
        name="ssd_mixer",
    )(x2, x2, mix_w.reshape(1, d), w_all, conv_w, conv_b.reshape(1, cd),
      jnp.pad(dt_bias, (0, pad)).reshape(1, LANES), jnp.pad(a_log, (0, pad)).reshape(1, LANES),
      jnp.repeat(d_skip, SSD_HEADDIM).reshape(1, d_inner), norm_w.reshape(1, d_inner), w_out.astype(BF16))
    return out.reshape(bsz, seq, d)


def _router_kernel(x_ref, nw_ref, rw_ref, idx_ref, gate_ref):
    h = _rms_rows(x_ref[...], nw_ref[...])
    hi = h.astype(BF16)
    lo = (h - hi.astype(F32)).astype(BF16)
    both = _dot(hi, rw_ref[...])
    logits = both[:, :LANES] + both[:, LANES:] + _dot(lo, rw_ref[:, :LANES])
    lane = lax.broadcasted_iota(jnp.int32, logits.shape, 1)
    valid = lane < MOE_EXPERTS
    lg = jnp.where(valid, logits, -jnp.inf)
    e = jnp.exp(lg - jnp.max(lg, axis=-1, keepdims=True))
    p = jnp.where(valid, e / jnp.sum(e, axis=-1, keepdims=True), -1.0)
    m1 = jnp.max(p, axis=-1, keepdims=True)
    i1 = jnp.min(jnp.where(p == m1, lane, LANES), axis=-1, keepdims=True)
    p2 = jnp.where(lane == i1, -1.0, p)
    m2 = jnp.max(p2, axis=-1, keepdims=True)
    i2 = jnp.min(jnp.where(p2 == m2, lane, LANES), axis=-1, keepdims=True)
    s = m1 + m2
    idx_ref[...] = jnp.where(lane == 0, i1, jnp.where(lane == 1, i2, 0))
    gate_ref[...] = jnp.where(lane == 0, m1 / s, jnp.where(lane == 1, m2 / s, 0.0))


def _router(x2, norm_w, router_w):
    t, d = x2.shape
    ne = router_w.shape[1]
    assert ne == MOE_EXPERTS and t % ROUTER_TOKENS == 0
    rw = jnp.pad(router_w, ((0, 0), (0, LANES - ne)))
    rw_hi = rw.astype(BF16)
    rw_lo = (rw - rw_hi.astype(F32)).astype(BF16)
    rw = jnp.concatenate([rw_hi, rw_lo], axis=1)
    return pl.pallas_call(
        _router_kernel,
        grid=(t // ROUTER_TOKENS,),
        in_specs=[
            pl.BlockSpec((ROUTER_TOKENS, d), lambda i: (i, 0)),
            pl.BlockSpec((1, d), lambda i: (0, 0)),
            pl.BlockSpec((d, 2 * LANES), lambda i: (0, 0)),
        ],
        out_specs=[pl.BlockSpec((ROUTER_TOKENS, LANES), lambda i: (i, 0))] * 2,
        out_shape=[jax.ShapeDtypeStruct((t, LANES), jnp.int32), jax.ShapeDtypeStruct((t, LANES), F32)],
        compiler_params=pltpu.CompilerParams(dimension_semantics=("arbitrary",)),
        name="moe_router",
    )(x2, norm_w.reshape(1, d), rw)


def _for_run_pieces(length, piece_fn):
    off = 0
    for size in MOE_RUN_PIECES:
        take = (length & size) != 0

        @pl.when(take)
        def _(off=off, size=size):
            piece_fn(off, size)

        off = off + jnp.where(take, size, 0)


def _aligned(x):
    return pl.multiple_of(x, MOE_RUN_ALIGN)


def _pair_one_hot(q_col, n):
    lane = lax.broadcasted_iota(jnp.int32, (q_col.shape[0], n), 1)
    return jnp.where(lane == q_col, 1.0, 0.0)


def _dispatch_kernel(run_ref, pad_ref, x_ref, q_ref, nw_ref, xs_hbm, stage, zbuf, sem, pad_sem):
    i = pl.program_id(0)
    last = pl.num_programs(0) - 1
    npair = stage.shape[0]
    ne = MOE_EXPERTS

    h = _rms_rows(x_ref[...], nw_ref[...]).astype(BF16)
    q = q_ref[...]
    sel = (_pair_one_hot(q[:, 0:1], npair) + _pair_one_hot(q[:, 1:2], npair)).astype(BF16)
    ordered = _dot_tn(sel, h)

    def block_runs(blk, act):
        base = blk * (3 * ne)
        for e in range(ne):
            dst0 = run_ref[base + e]
            loc0 = run_ref[base + 2 * ne + e]

            def piece(off, size, dst0=dst0, loc0=loc0):
                act(pltpu.make_async_copy(stage.at[pl.ds(_aligned(loc0 + off), size), :],
                                          xs_hbm.at[pl.ds(_aligned(dst0 + off), size), :], sem))

            _for_run_pieces(run_ref[base + ne + e], piece)

    @pl.when(i > 0)
    def _():
        block_runs(i - 1, lambda c: c.wait())

    stage[...] = ordered
    block_runs(i, lambda c: c.start())

    @pl.when(i == last)
    def _():
        block_runs(i, lambda c: c.wait())
        zbuf[...] = jnp.zeros_like(zbuf)
        for e in range(ne):
            first = pad_ref[e]
            count = pad_ref[ne + e]

            def zero_issue(r, carry, first=first):
                pltpu.make_async_copy(zbuf.at[pl.ds(0, 1), :], xs_hbm.at[pl.ds(first + r, 1), :], pad_sem).start()
                return carry

            def zero_wait(r, carry, first=first):
                pltpu.make_async_copy(zbuf.at[pl.ds(0, 1), :], xs_hbm.at[pl.ds(first, 1), :], pad_sem).wait()
                return carry

            lax.fori_loop(0, count, zero_issue, 0)
            lax.fori_loop(0, count, zero_wait, 0)

        tm = zbuf.shape[0]
        first_tile = pad_ref[2 * ne]

        def tile_copy(j):
            return pltpu.make_async_copy(zbuf, xs_hbm.at[pl.ds((first_tile + j) * tm, tm), :], pad_sem)

        def tile_issue(j, carry):
            tile_copy(j).start()
            return carry

        def tile_wait(j, carry):
            tile_copy(j).wait()
            return carry

        n_unused = xs_hbm.shape[0] // tm - first_tile
        lax.fori_loop(0, n_unused, tile_issue, 0)
        lax.fori_loop(0, n_unused, tile_wait, 0)


def _dispatch(x2, norm_w, q2, run_info, pad_info, n_rows):
    t, d = x2.shape
    tb = COMBINE_TOKENS
    assert t % tb == 0
    grid_spec = pltpu.PrefetchScalarGridSpec(
        num_scalar_prefetch=2,
        grid=(t // tb,),
        in_specs=[
            pl.BlockSpec((tb, d), lambda i, run, pad: (i, 0)),
            pl.BlockSpec((tb, 2), lambda i, run, pad: (i, 0)),
            pl.BlockSpec((1, d), lambda i, run, pad: (0, 0)),
        ],
        out_specs=pl.BlockSpec(memory_space=pl.ANY),
        scratch_shapes=[pltpu.VMEM((MOE_STAGE_ROWS, d), F32), pltpu.VMEM((MOE_TILE, d), F32),
                        pltpu.SemaphoreType.DMA(()), pltpu.SemaphoreType.DMA(())],
    )
    return pl.pallas_call(
        _dispatch_kernel,
        grid_spec=grid_spec,
        out_shape=jax.ShapeDtypeStruct((n_rows, d), F32),
        compiler_params=pltpu.CompilerParams(dimension_semantics=("arbitrary",)),
        name="moe_dispatch",
    )(run_info, pad_info, x2, q2, norm_w.reshape(1, d))


def _experts_kernel(te_ref, tv_ref, ts_ref, xs_ref, wg_ref, wu_ref, wd_ref, y_ref):
    i = pl.program_id(0)

    @pl.when(tv_ref[i] > 0)
    def _():
        h = xs_ref[...].astype(BF16)
        y_ref[...] = _swiglu_rows(h, wg_ref.at[0], wu_ref.at[0], wd_ref.at[0], None)

    @pl.when(tv_ref[i] == 0)
    def _():
        y_ref[...] = jnp.zeros_like(y_ref)


def _experts(xs, tile_expert, tile_valid, tile_src, w_gate, w_up, w_down):
    n_rows, d = xs.shape
    ne, _, dff = w_gate.shape
    tm = MOE_TILE
    nt = n_rows // tm
    wmap = lambda i, te, tv, ts: (te[i], 0, 0)
    grid_spec = pltpu.PrefetchScalarGridSpec(
        num_scalar_prefetch=3,
        grid=(nt,),
        in_specs=[
            pl.BlockSpec((tm, d), lambda i, te, tv, ts: (ts[i], 0)),
            pl.BlockSpec((1, d, dff), wmap),
            pl.BlockSpec((1, d, dff), wmap),
            pl.BlockSpec((1, dff, d), wmap),
        ],
        out_specs=pl.BlockSpec((tm, d), lambda i, te, tv, ts: (i, 0)),
    )
    return pl.pallas_call(
        _experts_kernel,
        grid_spec=grid_spec,
        out_shape=jax.ShapeDtypeStruct((n_rows, d), F32),
        compiler_params=pltpu.CompilerParams(
            dimension_semantics=("arbitrary",), vmem_limit_bytes=VMEM_LIMIT),
        name="moe_experts",
    )(tile_expert, tile_valid, tile_src, xs, w_gate.astype(BF16), w_up.astype(BF16), w_down.astype(BF16))


def _combine_kernel(run_ref, x_ref, q_ref, gate_ref, fw_ref, y_hbm, o_ref, ybuf, sem):
    i = pl.program_id(0)
    npair = ybuf.shape[1]
    ne = MOE_EXPERTS
    slot = lax.rem(i, 2)

    def block_runs(blk, s, act, clear_tail=False):
        if clear_tail:
            tail = 2 * x_ref.shape[0]
            ybuf[s, tail:, :] = jnp.zeros((npair - tail, ybuf.shape[2]), F32)
        base = blk * (3 * ne)
        for e in range(ne):
            src0 = run_ref[base + e]
            loc0 = run_ref[base + 2 * ne + e]

            def piece(off, size, src0=src0, loc0=loc0):
                act(pltpu.make_async_copy(y_hbm.at[pl.ds(_aligned(src0 + off), size), :],
                                          ybuf.at[s, pl.ds(_aligned(loc0 + off), size), :], sem.at[s]))

            _for_run_pieces(run_ref[base + ne + e], piece)

    @pl.when(i == 0)
    def _():
        block_runs(0, 0, lambda c: c.start(), clear_tail=True)

    @pl.when(i + 1 < pl.num_programs(0))
    def _():
        block_runs(i + 1, 1 - slot, lambda c: c.start(), clear_tail=True)

    block_runs(i, slot, lambda c: c.wait())
    yb = ybuf[slot].astype(BF16)
    q = q_ref[...]
    g = gate_ref[...]
    y0 = _dot(_pair_one_hot(q[:, 0:1], npair).astype(BF16), yb)
    y1 = _dot(_pair_one_hot(q[:, 1:2], npair).astype(BF16), yb)
    o = x_ref[...] + g[:, 0:1] * y0 + g[:, 1:2] * y1
    o_ref[...] = _rms_rows(o, fw_ref[...])


def _combine(x2, final_w, q2, run_info, gate_pad, y_rows):
    t, d = x2.shape
    tb = COMBINE_TOKENS
    nb = t // tb
    grid_spec = pltpu.PrefetchScalarGridSpec(
        num_scalar_prefetch=1,
        grid=(nb,),
        in_specs=[
            pl.BlockSpec((tb, d), lambda i, run: (i, 0)),
            pl.BlockSpec((tb, 2), lambda i, run: (i, 0)),
            pl.BlockSpec((tb, LANES), lambda i, run: (i, 0)),
            pl.BlockSpec((1, d), lambda i, run: (0, 0)),
            pl.BlockSpec(memory_space=pl.ANY),
        ],
        out_specs=pl.BlockSpec((tb, d), lambda i, run: (i, 0)),
        scratch_shapes=[pltpu.VMEM((2, MOE_STAGE_ROWS, d), F32), pltpu.SemaphoreType.DMA((2,))],
    )
    return pl.pallas_call(
        _combine_kernel,
        grid_spec=grid_spec,
        out_shape=jax.ShapeDtypeStruct(x2.shape, F32),
        compiler_params=pltpu.CompilerParams(dimension_semantics=("arbitrary",)),
        name="moe_combine",
    )(run_info, x2, q2, gate_pad, final_w.reshape(1, d), y_rows)


def _moe_layer_and_final_norm(x2, norm_w, final_w, router_w, w_gate, w_up, w_down):
    t, d = x2.shape
    ne = MOE_EXPERTS
    tm = MOE_TILE
    tb = COMBINE_TOKENS
    nb = t // tb
    npair = 2 * tb
    idx_pad, gate_pad = _router(x2, norm_w, router_w)
    flat_e = idx_pad[:, :2].reshape(-1)
    onehot = (flat_e[:, None] == jnp.arange(ne, dtype=jnp.int32)[None, :]).astype(jnp.int32)
    csum = jnp.cumsum(onehot, axis=0)
    block_end = csum.reshape(nb, npair, ne)[:, -1, :]
    block_begin = jnp.concatenate([jnp.zeros((1, ne), jnp.int32), block_end[:-1]], axis=0)
    al = MOE_RUN_ALIGN
    run_len = ((block_end - block_begin + al - 1) // al) * al
    run_end = jnp.cumsum(run_len, axis=0)
    counts = run_end[-1]
    padded = ((counts + tm - 1) // tm) * tm
    ends = jnp.cumsum(padded)
    starts = ends - padded
    run_row = starts[None, :] + run_end - run_len
    run_loc = jnp.cumsum(run_len, axis=1) - run_len
    run_info = jnp.concatenate([run_row, run_len, run_loc], axis=1).reshape(-1).astype(jnp.int32)
    local = (run_loc - block_begin)[:, None, :] + csum.reshape(nb, npair, ne) - 1
    q2 = jnp.sum(local * onehot.reshape(nb, npair, ne), axis=2).reshape(t, 2).astype(jnp.int32)
    nt = (2 * t + nb * ne * al) // tm + ne
    tile_ids = jnp.arange(nt, dtype=jnp.int32)
    tile_valid = (tile_ids * tm < ends[-1]).astype(jnp.int32)
    tile_expert = jnp.minimum(jnp.sum((tile_ids[:, None] * tm >= ends[None, :]).astype(jnp.int32), axis=1), ne - 1)
    tile_src = tile_ids * tile_valid
    pad_info = jnp.concatenate([starts + counts, padded - counts, ends[-1:] // tm]).astype(jnp.int32)
    xs = _dispatch(x2, norm_w, q2, run_info, pad_info, nt * tm)
    y_rows = _experts(xs, tile_expert.astype(jnp.int32), tile_valid, tile_src, w_gate, w_up, w_down)
    return _combine(x2, final_w, q2, run_info, gate_pad, y_rows)


def kernel(x, mix_norm_w, ffn_norm_w, final_norm_w, hg_w_in, hg_lb_logits, hg_norm_w, hg_w_out, ssd_w_in, ssd_conv_w, ssd_conv_b, ssd_dt_bias, ssd_a_log, ssd_d, ssd_norm_w, ssd_w_out, ffn_w_gate, ffn_w_up, ffn_w_down, moe_router, moe_w_gate, moe_w_up, moe_w_down):
    bsz, seq, d = x.shape
    assert mix_norm_w.shape[0] == 2 and hg_w_in.shape[0] == 1 and ssd_w_in.shape[0] == 1
    x = _hgrn_layer(x, mix_norm_w[0], hg_w_in[0], hg_lb_logits, hg_norm_w[0], hg_w_out[0])
    x2 = _ffn_layer(x.reshape(bsz * seq, d), ffn_norm_w[0], ffn_w_gate[0], ffn_w_up[0], ffn_w_down[0])
    x = _ssd_layer(x2.reshape(bsz, seq, d), mix_norm_w[1], ssd_w_in[0], ssd_conv_w[0], ssd_conv_b[0],
                   ssd_dt_bias[0], ssd_a_log[0], ssd_d[0], ssd_norm_w[0], ssd_w_out[0])
    out = _moe_layer_and_final_norm(x.reshape(bsz * seq, d), ffn_norm_w[1], final_norm_w, moe_router[0],
                                    moe_w_gate[0], moe_w_up[0], moe_w_down[0])
    return out.reshape(bsz, seq, d)
```

```python
import functools

import jax
import jax.numpy as jnp
from jax import lax
from jax.experimental import pallas as pl
from jax.experimental.pallas import tpu as pltpu

F32 = jnp.float32
BF16 = jnp.bfloat16
EPS = 1e-6
LOG2_E = 1.4426950408889634
LANES = 128
VMEM_LIMIT = 60000 * 1024

HG_CHUNK = 256
HG_LEVELS = (128, 64, 32, 16, 8, 4, 2, 1)
HG_TOKENS = 256
SSD_CHUNK = 128
SSD_BLOCK = 256
SSD_PROJ_SLAB = 256
SSD_HEADDIM = 64
SSD_STATE = 128
SSD_GROUPS = 4
CONV_W = 4
CONV_TAIL = 8
FFN_TOKENS = 1024
FFN_CHUNK = 2816
MOE_EXPERTS = 8
MOE_TILE = 512
ROUTER_TOKENS = 512
COMBINE_TOKENS = 256
MOE_RUN_ALIGN = 8
MOE_RUN_PIECES = tuple(1 << k for k in range(9, 2, -1))
MOE_RUN_CHUNK = 32
MOE_STAGE_ROWS = 640
MOE_RUN_FIELDS = 3 * MOE_EXPERTS + 1


def _dot(a, b):
    return jnp.dot(a, b, preferred_element_type=F32)


def _dot_nt(a, b):
    return lax.dot_general(a, b, (((1,), (1,)), ((), ())), preferred_element_type=F32)


def _dot_tn(a, b):
    return lax.dot_general(a, b, (((0,), (0,)), ((), ())), preferred_element_type=F32)


def _sigmoid(x):
    return 0.5 * jnp.tanh(0.5 * x) + 0.5


def _rms_rows(x, w):
    return x * lax.rsqrt(jnp.mean(x * x, axis=-1, keepdims=True) + EPS) * w


def _tri_ones(n):
    r = lax.broadcasted_iota(jnp.int32, (n, n), 0)
    c = lax.broadcasted_iota(jnp.int32, (n, n), 1)
    return jnp.where(r >= c, 1.0, 0.0).astype(BF16)


def _cumsum_rows(tri, x):
    hi = x.astype(BF16)
    r1 = x - hi.astype(F32)
    mid = r1.astype(BF16)
    lo = (r1 - mid.astype(F32)).astype(BF16)
    return _dot(tri, hi) + _dot(tri, mid) + _dot(tri, lo)


def _hgrn_kernel(x_ref, mw_ref, win_ref, lbl_ref, nw_ref, wout_ref, o_ref, proj_s, st_s, *, nh, dk, dv):
    tc = x_ref.shape[1]
    qk = nh * dk
    vd = nh * dv

    @pl.when(pl.program_id(1) == 0)
    def _():
        st_s[...] = jnp.zeros_like(st_s)

    x = x_ref[0]
    h = _rms_rows(x, mw_ref[...]).astype(BF16)
    for r0 in range(0, tc, HG_CHUNK):
        proj_s[r0:r0 + HG_CHUNK, :] = _dot(h[r0:r0 + HG_CHUNK], win_ref[...])

    lbl = lbl_ref[...]
    lmax = jnp.max(lbl, axis=0, keepdims=True)
    le = jnp.exp(lbl - lmax)
    lb = le[0:1, :] / jnp.sum(le, axis=0, keepdims=True)

    tri = _tri_ones(HG_CHUNK)
    half = HG_CHUNK // 2
    assert HG_LEVELS[0] == half
    ti = lax.broadcasted_iota(jnp.int32, (half, half), 0)
    si = lax.broadcasted_iota(jnp.int32, (half, half), 1)
    level = jnp.where(si > ti, -2, 31 - lax.clz(ti ^ si))
    level_ids = [hs.bit_length() - 1 for hs in HG_LEVELS[1:]] + [-1]
    sub8 = lax.broadcasted_iota(jnp.int32, (HG_CHUNK // 8, 8, qk), 1)

    def level_decay(b, hs):
        if hs >= 8:
            parts = []
            for lo in range(0, HG_CHUNK, 2 * hs):
                r = b[lo + hs - 1:lo + hs, :]
                parts += [r - b[lo:lo + hs], b[lo + hs:lo + 2 * hs] - r]
            return jnp.exp2(jnp.concatenate(parts, axis=0)).astype(BF16)
        b3 = b.reshape(HG_CHUNK // 8, 8, qk)
        ref = jnp.broadcast_to(b3[:, hs - 1:hs, :], b3.shape)
        for lo in range(2 * hs, 8, 2 * hs):
            ref = jnp.where(sub8 >= lo, jnp.broadcast_to(b3[:, lo + hs - 1:lo + hs, :], b3.shape), ref)
        return jnp.exp2(-jnp.abs(b - ref.reshape(HG_CHUNK, qk))).astype(BF16)

    def chunk_body(c):
        r0 = c * HG_CHUNK
        q = proj_s[pl.ds(r0, HG_CHUNK), 0:qk]
        fp = proj_s[pl.ds(r0, HG_CHUNK), qk:2 * qk]
        v = proj_s[pl.ds(r0, HG_CHUNK), 2 * qk:2 * qk + vd]
        f = lb + (1.0 - lb) * _sigmoid(fp)
        k = 1.0 - f
        b = _cumsum_rows(tri, jnp.log2(f))
        b_last = b[HG_CHUNK - 1:HG_CHUNK, :]
        qg = (q * jnp.exp2(b)).astype(BF16)
        kdec = (k * jnp.exp2(b_last - b)).astype(BF16)
        g_last = jnp.exp2(b_last)
        vb = v.astype(BF16)

        qb = q.astype(BF16)
        kb = k.astype(BF16)
        z_top = level_decay(b, half)
        qd_top = qb[half:] * z_top[half:]
        kd_top = kb[:half] * z_top[:half]
        qds = []
        kds = []
        for hs in HG_LEVELS[1:]:
            z = level_decay(b, hs)
            qds.append(qb * z)
            kds.append(kb * z)
        qds.append(qb)
        kds.append(kb)

        def diagonal_block(rows, ks):
            attn = jnp.zeros((half, half), F32)
            for qd, kd, lid in zip(qds, kds, level_ids):
                attn = jnp.where(level == lid, _dot_nt(qd[rows, ks], kd[rows, ks]), attn)
            return attn

        o_heads = []
        for hh in range(nh):
            ks = slice(hh * dk, (hh + 1) * dk)
            vh = vb[:, hh * dv:(hh + 1) * dv]
            attn_lo = diagonal_block(slice(0, half), ks)
            attn_hi = jnp.concatenate([_dot_nt(qd_top[:, ks], kd_top[:, ks]), diagonal_block(slice(half, HG_CHUNK), ks)],
                                      axis=1)
            o_intra = jnp.concatenate([_dot(attn_lo.astype(BF16), vh[:half]), _dot(attn_hi.astype(BF16), vh)], axis=0)
            st = st_s[hh]
            o_inter = _dot_nt(qg[:, ks], st.astype(BF16))
            oh = o_inter + o_intra
            o_heads.append(oh * lax.rsqrt(jnp.mean(oh * oh, axis=-1, keepdims=True) + EPS))
            st_s[hh] = st * g_last[:, ks] + _dot_tn(vb[:, hh * dv:(hh + 1) * dv], kdec[:, ks])
        g = proj_s[pl.ds(r0, HG_CHUNK), 2 * qk + vd:2 * qk + 2 * vd]
        on = jnp.concatenate(o_heads, axis=1) * nw_ref[...] * _sigmoid(g)
        o_ref[0, pl.ds(r0, HG_CHUNK), :] = x[r0:r0 + HG_CHUNK] + _dot(on.astype(BF16), wout_ref[...])

    for c in range(tc // HG_CHUNK):
        chunk_body(c)


def _hgrn_layer(x, mix_w, w_in, lb_logits, norm_w, w_out):
    bsz, seq, d = x.shape
    dv = norm_w.shape[0]
    nh = w_out.shape[0] // dv
    dk = (w_in.shape[1] - 2 * nh * dv) // (2 * nh)
    assert lb_logits.shape[0] == 2 and dk == LANES and dv == LANES
    tc = HG_TOKENS
    assert seq % tc == 0 and tc % HG_CHUNK == 0
    const = lambda b, j: (0, 0)
    return pl.pallas_call(
        functools.partial(_hgrn_kernel, nh=nh, dk=dk, dv=dv),
        grid=(bsz, seq // tc),
        in_specs=[
            pl.BlockSpec((1, tc, d), lambda b, j: (b, j, 0)),
            pl.BlockSpec((1, d), const),
            pl.BlockSpec(w_in.shape, const, pipeline_mode=pl.Buffered(1)),
            pl.BlockSpec(lb_logits.shape, const),
            pl.BlockSpec((1, nh * dv), const),
            pl.BlockSpec(w_out.shape, const, pipeline_mode=pl.Buffered(1)),
        ],
        out_specs=pl.BlockSpec((1, tc, d), lambda b, j: (b, j, 0)),
        out_shape=jax.ShapeDtypeStruct(x.shape, F32),
        scratch_shapes=[
            pltpu.VMEM((tc, w_in.shape[1]), F32),
            pltpu.VMEM((nh, dv, dk), F32),
        ],
        compiler_params=pltpu.CompilerParams(
            dimension_semantics=("arbitrary", "arbitrary"), vmem_limit_bytes=VMEM_LIMIT),
        name="hgrn_mixer",
    )(x, mix_w.reshape(1, d), w_in.astype(BF16), lb_logits, jnp.tile(norm_w, nh).reshape(1, nh * dv),
      w_out.astype(BF16))


def _swiglu_rows(h, wg_ref, wu_ref, wd_ref, acc):
    dff = wg_ref.shape[-1]
    for c0 in range(0, dff, FFN_CHUNK):
        g = _dot(h, wg_ref[:, c0:c0 + FFN_CHUNK])
        u = _dot(h, wu_ref[:, c0:c0 + FFN_CHUNK])
        a = (g * _sigmoid(g) * u).astype(BF16)
        y = _dot(a, wd_ref[c0:c0 + FFN_CHUNK, :])
        acc = y if acc is None else acc + y
    return acc


def _ffn_kernel(x_ref, nw_ref, wg_ref, wu_ref, wd_ref, o_ref):
    x = x_ref[...]
    h = _rms_rows(x, nw_ref[...]).astype(BF16)
    o_ref[...] = _swiglu_rows(h, wg_ref, wu_ref, wd_ref, x)


def _ffn_layer(x2, norm_w, w_gate, w_up, w_down):
    t, d = x2.shape
    dff = w_gate.shape[1]
    assert t % FFN_TOKENS == 0 and dff % FFN_CHUNK == 0
    const = lambda i: (0, 0)
    return pl.pallas_call(
        _ffn_kernel,
        grid=(t // FFN_TOKENS,),
        in_specs=[
            pl.BlockSpec((FFN_TOKENS, d), lambda i: (i, 0)),
            pl.BlockSpec((1, d), const),
            pl.BlockSpec((d, dff), const, pipeline_mode=pl.Buffered(1)),
            pl.BlockSpec((d, dff), const, pipeline_mode=pl.Buffered(1)),
            pl.BlockSpec((dff, d), const, pipeline_mode=pl.Buffered(1)),
        ],
        out_specs=pl.BlockSpec((FFN_TOKENS, d), lambda i: (i, 0)),
        out_shape=jax.ShapeDtypeStruct(x2.shape, F32),
        compiler_params=pltpu.CompilerParams(
            dimension_semantics=("arbitrary",), vmem_limit_bytes=VMEM_LIMIT),
        name="dense_ffn",
    )(x2, norm_w.reshape(1, d), w_gate.astype(BF16), w_up.astype(BF16), w_down.astype(BF16))


def _ssd_mix_block(proj_s, r0, side, x_ref, cw_ref, cb_ref, dtb_ref, alog_ref, dsk_ref, nw_ref, wout_ref, o_ref,
                   cbuf, y_s, st_s, *, d_inner):
    tc = SSD_BLOCK
    p2 = 2 * SSD_HEADDIM
    assert p2 == LANES and SSD_STATE == LANES
    gs = SSD_GROUPS * SSD_STATE
    cd = d_inner + 2 * gs
    pairs_per_group = d_inner // (SSD_GROUPS * p2)
    q = SSD_CHUNK
    assert q == LANES
    n_points = cd // LANES + (tc // q) * SSD_GROUPS * pairs_per_group
    emitted = [0, 0]

    def interleave():
        emitted[0] += 1
        while emitted[1] < len(side) and emitted[1] * n_points < emitted[0] * len(side):
            side[emitted[1]]()
            emitted[1] += 1

    for j in range(cd // LANES):
        c0 = d_inner + j * LANES
        cbuf[j, CONV_TAIL:CONV_TAIL + tc, :] = proj_s[:, c0:c0 + LANES]
        conv = cb_ref[:, j * LANES:(j + 1) * LANES]
        for k in range(CONV_W):
            off = CONV_TAIL - (CONV_W - 1) + k
            conv = conv + cw_ref[k:k + 1, j * LANES:(j + 1) * LANES] * cbuf[j, off:off + tc, :]
        cbuf[j, 0:CONV_TAIL, :] = cbuf[j, tc:tc + CONV_TAIL, :]
        proj_s[:, c0:c0 + LANES] = conv * _sigmoid(conv)
        interleave()

    dt_pre = proj_s[:, d_inner + cd:d_inner + cd + LANES] + dtb_ref[...]
    dt = jnp.maximum(dt_pre, 0.0) + jnp.log1p(jnp.exp(-jnp.abs(dt_pre)))
    da = dt * (-LOG2_E * jnp.exp(alog_ref[...]))

    tri = _tri_ones(q)
    rr = lax.broadcasted_iota(jnp.int32, (q, q), 0)
    cc = lax.broadcasted_iota(jnp.int32, (q, q), 1)
    causal = rr >= cc
    first = lax.broadcasted_iota(jnp.int32, (1, LANES), 1) < SSD_HEADDIM

    def lane_col(a, hd):
        return jnp.broadcast_to(a[:, hd:hd + 1], (q, LANES))

    for sc in range(tc // q):
        lo, hi = sc * q, (sc + 1) * q
        dtc = dt[lo:hi]
        cs = _cumsum_rows(tri, da[lo:hi])
        last = cs[q - 1:q, :]
        cs_t = cs.T
        for g in range(SSD_GROUPS):
            bm = proj_s[lo:hi, 2 * d_inner + g * SSD_STATE:2 * d_inner + (g + 1) * SSD_STATE]
            cm = proj_s[lo:hi, 2 * d_inner + gs + g * SSD_STATE:2 * d_inner + gs + (g + 1) * SSD_STATE]
            bmb = bm.astype(BF16)
            cmb = cm.astype(BF16)
            cb = _dot_nt(cmb, bmb)
            for pr in range(pairs_per_group):
                pi = g * pairs_per_group + pr
                h1, h2 = 2 * pi, 2 * pi + 1
                xp = proj_s[lo:hi, d_inner + pi * p2:d_inner + (pi + 1) * p2]
                c1 = lane_col(cs, h1)
                c2 = lane_col(cs, h2)
                cs_p = jnp.where(first, c1, c2)
                last_p = jnp.where(first, last[:, h1:h1 + 1], last[:, h2:h2 + 1])
                xdt = jnp.where(first, lane_col(dtc, h1), lane_col(dtc, h2)) * xp
                m1 = cb * jnp.exp2(jnp.where(causal, c1 - cs_t[h1:h1 + 1, :], -jnp.inf))
                m2 = cb * jnp.exp2(jnp.where(causal, c2 - cs_t[h2:h2 + 1, :], -jnp.inf))
                x1 = jnp.where(first, xdt, 0.0).astype(BF16)
                x2 = jnp.where(first, 0.0, xdt).astype(BF16)
                y = _dot(m1.astype(BF16), x1) + _dot(m2.astype(BF16), x2)
                st = st_s[pi]
                y = y + jnp.exp2(cs_p) * _dot(cmb, st.astype(BF16))
                wx = (jnp.exp2(last_p - cs_p) * xdt).astype(BF16)
                st_s[pi] = st * jnp.exp2(last_p) + _dot_tn(bmb, wx)
                y_s[lo:hi, pi * p2:(pi + 1) * p2] = y + dsk_ref[:, pi * p2:(pi + 1) * p2] * xp
                interleave()

    z = proj_s[:, 0:d_inner]
    y = y_s[...] * (z * _sigmoid(z))
    gw = d_inner // SSD_GROUPS
    parts = []
    for g in range(SSD_GROUPS):
        yg = y[:, g * gw:(g + 1) * gw]
        parts.append(yg * lax.rsqrt(jnp.mean(yg * yg, axis=-1, keepdims=True) + EPS))
    yn = jnp.concatenate(parts, axis=1) * nw_ref[...]
    o_ref[r0:r0 + tc, :] = x_ref[r0:r0 + tc, :] + _dot(yn.astype(BF16), wout_ref[...])


def _ssd_kernel(x_ref, xn_ref, mw_ref, win_ref, cw_ref, cb_ref, dtb_ref, alog_ref, dsk_ref, nw_ref, wout_ref, o_ref,
                proj_a, proj_b, cbuf, y_s, st_s, *, d_inner, steps_per_seq):
    tb = SSD_BLOCK
    n_all = win_ref.shape[1]
    step = pl.program_id(0)

    @pl.when(lax.rem(step, steps_per_seq) == 0)
    def _():
        st_s[...] = jnp.zeros_like(st_s)
        cbuf[:, 0:CONV_TAIL, :] = jnp.zeros((cbuf.shape[0], CONV_TAIL, LANES), F32)

    def projection_pieces(rows_ref, r0, dst):
        cache = []

        def piece(c0, c1):
            if not cache:
                cache.append(_rms_rows(rows_ref[r0:r0 + tb, :], mw_ref[...]).astype(BF16))
            dst[:, c0:c1] = _dot(cache[0], win_ref[:, c0:c1])

        return [functools.partial(piece, c0, min(c0 + SSD_PROJ_SLAB, n_all)) for c0 in range(0, n_all, SSD_PROJ_SLAB)]

    @pl.when(step == 0)
    def _():
        for piece in projection_pieces(x_ref, 0, proj_a):
            piece()

    mix = functools.partial(_ssd_mix_block, x_ref=x_ref, cw_ref=cw_ref, cb_ref=cb_ref, dtb_ref=dtb_ref,
                            alog_ref=alog_ref, dsk_ref=dsk_ref, nw_ref=nw_ref, wout_ref=wout_ref, o_ref=o_ref,
                            cbuf=cbuf, y_s=y_s, st_s=st_s, d_inner=d_inner)
    mix(proj_a, 0, projection_pieces(x_ref, tb, proj_b))
    mix(proj_b, tb, projection_pieces(xn_ref, 0, proj_a))


def _ssd_layer(x, mix_w, w_in, conv_w, conv_b, dt_bias, a_log, d_skip, norm_w, w_out):
    bsz, seq, d = x.shape
    d_inner = w_out.shape[0]
    nheads = dt_bias.shape[0]
    assert d_inner == nheads * SSD_HEADDIM and nheads <= LANES
    cd = d_inner + 2 * SSD_GROUPS * SSD_STATE
    assert conv_w.shape == (CONV_W, cd) and w_in.shape[1] == d_inner + cd + nheads
    tb = SSD_BLOCK
    assert seq % (2 * tb) == 0 and tb % SSD_CHUNK == 0
    t = bsz * seq
    pad = LANES - nheads
    w_all = jnp.pad(w_in, ((0, 0), (0, pad))).astype(BF16)
    n_all = w_all.shape[1]
    const = lambda i: (0, 0)
    last_block = t // tb - 1
    x2 = x.reshape(t, d)
    out = pl.pallas_call(
        functools.partial(_ssd_kernel, d_inner=d_inner, steps_per_seq=seq // (2 * tb)),
        grid=(t // (2 * tb),),
        in_specs=[
            pl.BlockSpec((2 * tb, d), lambda i: (i, 0)),
            pl.BlockSpec((tb, d), lambda i: (jnp.minimum(2 * i + 2, last_block), 0)),
            pl.BlockSpec((1, d), const),
            pl.BlockSpec(w_all.shape, const, pipeline_mode=pl.Buffered(1)),
            pl.BlockSpec((CONV_W, cd), const),
            pl.BlockSpec((1, cd), const),
            pl.BlockSpec((1, LANES), const),
            pl.BlockSpec((1, LANES), const),
            pl.BlockSpec((1, d_inner), const),
            pl.BlockSpec((1, d_inner), const),
            pl.BlockSpec(w_out.shape, const, pipeline_mode=pl.Buffered(1)),
        ],
        out_specs=pl.BlockSpec((2 * tb, d), lambda i: (i, 0)),
        out_shape=jax.ShapeDtypeStruct((t, d), F32),
        scratch_shapes=[
            pltpu.VMEM((tb, n_all), F32),
            pltpu.VMEM((tb, n_all), F32),
            pltpu.VMEM((cd // LANES, tb + CONV_TAIL, LANES), F32),
            pltpu.VMEM((tb, d_inner), F32),
            pltpu.VMEM((nheads // 2, SSD_STATE, LANES), F32),
        ],
        compiler_params=pltpu.CompilerParams(
            dimension_semantics=("arbitrary",), vmem_limit_bytes=VMEM_LIMIT),
        name="ssd_mixer",
    )(x2, x2, mix_w.reshape(1, d), w_all, conv_w, conv_b.reshape(1, cd),
      jnp.pad(dt_bias, (0, pad)).reshape(1, LANES), jnp.pad(a_log, (0, pad)).reshape(1, LANES),
      jnp.repeat(d_skip, SSD_HEADDIM).reshape(1, d_inner), norm_w.reshape(1, d_inner), w_out.astype(BF16))
    return out.reshape(bsz, seq, d)


def _router_kernel(x_ref, nw_ref, rw_ref, idx_ref, gate_ref):
    h = _rms_rows(x_ref[...], nw_ref[...])
    hi = h.astype(BF16)
    lo = (h - hi.astype(F32)).astype(BF16)
    both = _dot(hi, rw_ref[...])
    logits = both[:, :LANES] + both[:, LANES:] + _dot(lo, rw_ref[:, :LANES])
    lane = lax.broadcasted_iota(jnp.int32, logits.shape, 1)
    valid = lane < MOE_EXPERTS
    lg = jnp.where(valid, logits, -jnp.inf)
    e = jnp.exp(lg - jnp.max(lg, axis=-1, keepdims=True))
    p = jnp.where(valid, e / jnp.sum(e, axis=-1, keepdims=True), -1.0)
    m1 = jnp.max(p, axis=-1, keepdims=True)
    i1 = jnp.min(jnp.where(p == m1, lane, LANES), axis=-1, keepdims=True)
    p2 = jnp.where(lane == i1, -1.0, p)
    m2 = jnp.max(p2, axis=-1, keepdims=True)
    i2 = jnp.min(jnp.where(p2 == m2, lane, LANES), axis=-1, keepdims=True)
    s = m1 + m2
    idx_ref[...] = jnp.where(lane == 0, i1, jnp.where(lane == 1, i2, 0))
    gate_ref[...] = jnp.where(lane == 0, m1 / s, jnp.where(lane == 1, m2 / s, 0.0))


def _router(x2, norm_w, router_w):
    t, d = x2.shape
    ne = router_w.shape[1]
    assert ne == MOE_EXPERTS and t % ROUTER_TOKENS == 0
    rw = jnp.pad(router_w, ((0, 0), (0, LANES - ne)))
    rw_hi = rw.astype(BF16)
    rw_lo = (rw - rw_hi.astype(F32)).astype(BF16)
    rw = jnp.concatenate([rw_hi, rw_lo], axis=1)
    return pl.pallas_call(
        _router_kernel,
        grid=(t // ROUTER_TOKENS,),
        in_specs=[
            pl.BlockSpec((ROUTER_TOKENS, d), lambda i: (i, 0)),
            pl.BlockSpec((1, d), lambda i: (0, 0)),
            pl.BlockSpec((d, 2 * LANES), lambda i: (0, 0)),
        ],
        out_specs=[pl.BlockSpec((ROUTER_TOKENS, LANES), lambda i: (i, 0))] * 2,
        out_shape=[jax.ShapeDtypeStruct((t, LANES), jnp.int32), jax.ShapeDtypeStruct((t, LANES), F32)],
        compiler_params=pltpu.CompilerParams(dimension_semantics=("arbitrary",)),
        name="moe_router",
    )(x2, norm_w.reshape(1, d), rw)


def _for_binary_pieces(length, sizes, piece_fn, off=0):
    for size in sizes:
        take = (length & size) != 0

        @pl.when(take)
        def _(off=off, size=size):
            piece_fn(off, size)

        off = off + jnp.where(take, size, 0)


def _for_run_pieces(length, piece_fn):
    n_chunks = length // MOE_RUN_CHUNK

    def chunk(c, carry):
        piece_fn(c * MOE_RUN_CHUNK, MOE_RUN_CHUNK)
        return carry

    lax.fori_loop(0, n_chunks, chunk, 0)
    _for_binary_pieces(length, [s for s in MOE_RUN_PIECES if s < MOE_RUN_CHUNK], piece_fn, off=n_chunks * MOE_RUN_CHUNK)


def _aligned(x):
    return pl.multiple_of(x, MOE_RUN_ALIGN)


def _pair_one_hot(q_col, n):
    lane = lax.broadcasted_iota(jnp.int32, (q_col.shape[0], n), 1)
    return jnp.where(lane == q_col, 1.0, 0.0)


def _dispatch_kernel(run_ref, pad_ref, x_ref, q_ref, nw_ref, xs_hbm, stage, zbuf, sem, pad_sem):
    i = pl.program_id(0)
    last = pl.num_programs(0) - 1
    npair = stage.shape[0]
    ne = MOE_EXPERTS

    h = _rms_rows(x_ref[...], nw_ref[...]).astype(BF16)
    q = q_ref[...]
    sel = (_pair_one_hot(q[:, 0:1], npair) + _pair_one_hot(q[:, 1:2], npair)).astype(BF16)
    ordered = _dot_tn(sel, h)

    def start_block(blk):
        base = blk * MOE_RUN_FIELDS
        for e in range(ne):
            dst0 = run_ref[base + e]
            loc0 = run_ref[base + 2 * ne + e]

            def piece(off, size, dst0=dst0, loc0=loc0):
                pltpu.make_async_copy(stage.at[pl.ds(_aligned(loc0 + off), size), :],
                                      xs_hbm.at[pl.ds(_aligned(dst0 + off), size), :], sem).start()

            _for_run_pieces(run_ref[base + ne + e], piece)

    def wait_block(blk):
        def piece(off, size):
            pltpu.make_async_copy(stage.at[pl.ds(0, size), :], xs_hbm.at[pl.ds(0, size), :], sem).wait()

        _for_binary_pieces(run_ref[blk * MOE_RUN_FIELDS + 3 * ne], MOE_RUN_PIECES, piece)

    @pl.when(i > 0)
    def _():
        wait_block(i - 1)

    stage[...] = ordered
    start_block(i)

    @pl.when(i == last)
    def _():
        wait_block(i)
        zbuf[...] = jnp.zeros_like(zbuf)
        for e in range(ne):
            first = pad_ref[e]
            count = pad_ref[ne + e]

            def zero_issue(r, carry, first=first):
                pltpu.make_async_copy(zbuf.at[pl.ds(0, 1), :], xs_hbm.at[pl.ds(first + r, 1), :], pad_sem).start()
                return carry

            def zero_wait(r, carry, first=first):
                pltpu.make_async_copy(zbuf.at[pl.ds(0, 1), :], xs_hbm.at[pl.ds(first, 1), :], pad_sem).wait()
                return carry

            lax.fori_loop(0, count, zero_issue, 0)
            lax.fori_loop(0, count, zero_wait, 0)

        tm = zbuf.shape[0]
        first_tile = pad_ref[2 * ne]

        def tile_copy(j):
            return pltpu.make_async_copy(zbuf, xs_hbm.at[pl.ds((first_tile + j) * tm, tm), :], pad_sem)

        def tile_issue(j, carry):
            tile_copy(j).start()
            return carry

        def tile_wait(j, carry):
            tile_copy(j).wait()
            return carry

        n_unused = xs_hbm.shape[0] // tm - first_tile
        lax.fori_loop(0, n_unused, tile_issue, 0)
        lax.fori_loop(0, n_unused, tile_wait, 0)


def _dispatch(x2, norm_w, q2, run_info, pad_info, n_rows):
    t, d = x2.shape
    tb = COMBINE_TOKENS
    assert t % tb == 0
    grid_spec = pltpu.PrefetchScalarGridSpec(
        num_scalar_prefetch=2,
        grid=(t // tb,),
        in_specs=[
            pl.BlockSpec((tb, d), lambda i, run, pad: (i, 0)),
            pl.BlockSpec((tb, 2), lambda i, run, pad: (i, 0)),
            pl.BlockSpec((1, d), lambda i, run, pad: (0, 0)),
        ],
        out_specs=pl.BlockSpec(memory_space=pl.ANY),
        scratch_shapes=[pltpu.VMEM((MOE_STAGE_ROWS, d), F32), pltpu.VMEM((MOE_TILE, d), F32),
                        pltpu.SemaphoreType.DMA(()), pltpu.SemaphoreType.DMA(())],
    )
    return pl.pallas_call(
        _dispatch_kernel,
        grid_spec=grid_spec,
        out_shape=jax.ShapeDtypeStruct((n_rows, d), F32),
        compiler_params=pltpu.CompilerParams(dimension_semantics=("arbitrary",)),
        name="moe_dispatch",
    )(run_info, pad_info, x2, q2, norm_w.reshape(1, d))


def _experts_kernel(te_ref, tv_ref, ts_ref, xs_ref, wg_ref, wu_ref, wd_ref, y_ref):
    i = pl.program_id(0)

    @pl.when(tv_ref[i] > 0)
    def _():
        h = xs_ref[...].astype(BF16)
        y_ref[...] = _swiglu_rows(h, wg_ref.at[0], wu_ref.at[0], wd_ref.at[0], None)

    @pl.when(tv_ref[i] == 0)
    def _():
        y_ref[...] = jnp.zeros_like(y_ref)


def _experts(xs, tile_expert, tile_valid, tile_src, w_gate, w_up, w_down):
    n_rows, d = xs.shape
    ne, _, dff = w_gate.shape
    tm = MOE_TILE
    nt = n_rows // tm
    wmap = lambda i, te, tv, ts: (te[i], 0, 0)
    grid_spec = pltpu.PrefetchScalarGridSpec(
        num_scalar_prefetch=3,
        grid=(nt,),
        in_specs=[
            pl.BlockSpec((tm, d), lambda i, te, tv, ts: (ts[i], 0)),
            pl.BlockSpec((1, d, dff), wmap),
            pl.BlockSpec((1, d, dff), wmap),
            pl.BlockSpec((1, dff, d), wmap),
        ],
        out_specs=pl.BlockSpec((tm, d), lambda i, te, tv, ts: (i, 0)),
    )
    return pl.pallas_call(
        _experts_kernel,
        grid_spec=grid_spec,
        out_shape=jax.ShapeDtypeStruct((n_rows, d), F32),
        compiler_params=pltpu.CompilerParams(
            dimension_semantics=("arbitrary",), vmem_limit_bytes=VMEM_LIMIT),
        name="moe_experts",
    )(tile_expert, tile_valid, tile_src, xs, w_gate.astype(BF16), w_up.astype(BF16), w_down.astype(BF16))


def _combine_kernel(run_ref, x_ref, q_ref, gate_ref, fw_ref, y_hbm, o_ref, ybuf, sem):
    i = pl.program_id(0)
    npair = ybuf.shape[1]
    ne = MOE_EXPERTS
    slot = lax.rem(i, 2)

    def start_block(blk, s):
        tail = 2 * x_ref.shape[0]
        ybuf[s, tail:, :] = jnp.zeros((npair - tail, ybuf.shape[2]), F32)
        base = blk * MOE_RUN_FIELDS
        for e in range(ne):
            src0 = run_ref[base + e]
            loc0 = run_ref[base + 2 * ne + e]

            def piece(off, size, src0=src0, loc0=loc0):
                pltpu.make_async_copy(y_hbm.at[pl.ds(_aligned(src0 + off), size), :],
                                      ybuf.at[s, pl.ds(_aligned(loc0 + off), size), :], sem.at[s]).start()

            _for_run_pieces(run_ref[base + ne + e], piece)

    def wait_block(blk, s):
        def piece(off, size):
            pltpu.make_async_copy(y_hbm.at[pl.ds(0, size), :], ybuf.at[s, pl.ds(0, size), :], sem.at[s]).wait()

        _for_binary_pieces(run_ref[blk * MOE_RUN_FIELDS + 3 * ne], MOE_RUN_PIECES, piece)

    @pl.when(i == 0)
    def _():
        start_block(0, 0)

    @pl.when(i + 1 < pl.num_programs(0))
    def _():
        start_block(i + 1, 1 - slot)

    wait_block(i, slot)
    yb = ybuf[slot].astype(BF16)
    q = q_ref[...]
    g = gate_ref[...]
    y0 = _dot(_pair_one_hot(q[:, 0:1], npair).astype(BF16), yb)
    y1 = _dot(_pair_one_hot(q[:, 1:2], npair).astype(BF16), yb)
    o = x_ref[...] + g[:, 0:1] * y0 + g[:, 1:2] * y1
    o_ref[...] = _rms_rows(o, fw_ref[...])


def _combine(x2, final_w, q2, run_info, gate_pad, y_rows):
    t, d = x2.shape
    tb = COMBINE_TOKENS
    nb = t // tb
    grid_spec = pltpu.PrefetchScalarGridSpec(
        num_scalar_prefetch=1,
        grid=(nb,),
        in_specs=[
            pl.BlockSpec((tb, d), lambda i, run: (i, 0)),
            pl.BlockSpec((tb, 2), lambda i, run: (i, 0)),
            pl.BlockSpec((tb, LANES), lambda i, run: (i, 0)),
            pl.BlockSpec((1, d), lambda i, run: (0, 0)),
            pl.BlockSpec(memory_space=pl.ANY),
        ],
        out_specs=pl.BlockSpec((tb, d), lambda i, run: (i, 0)),
        scratch_shapes=[pltpu.VMEM((2, MOE_STAGE_ROWS, d), F32), pltpu.SemaphoreType.DMA((2,))],
    )
    return pl.pallas_call(
        _combine_kernel,
        grid_spec=grid_spec,
        out_shape=jax.ShapeDtypeStruct(x2.shape, F32),
        compiler_params=pltpu.CompilerParams(dimension_semantics=("arbitrary",)),
        name="moe_combine",
    )(run_info, x2, q2, gate_pad, final_w.reshape(1, d), y_rows)


def _moe_layer_and_final_norm(x2, norm_w, final_w, router_w, w_gate, w_up, w_down):
    t, d = x2.shape
    ne = MOE_EXPERTS
    tm = MOE_TILE
    tb = COMBINE_TOKENS
    nb = t // tb
    npair = 2 * tb
    idx_pad, gate_pad = _router(x2, norm_w, router_w)
    flat_e = idx_pad[:, :2].reshape(-1)
    onehot = (flat_e[:, None] == jnp.arange(ne, dtype=jnp.int32)[None, :]).astype(jnp.int32)
    csum = jnp.cumsum(onehot, axis=0)
    block_end = csum.reshape(nb, npair, ne)[:, -1, :]
    block_begin = jnp.concatenate([jnp.zeros((1, ne), jnp.int32), block_end[:-1]], axis=0)
    al = MOE_RUN_ALIGN
    run_len = ((block_end - block_begin + al - 1) // al) * al
    run_end = jnp.cumsum(run_len, axis=0)
    counts = run_end[-1]
    padded = ((counts + tm - 1) // tm) * tm
    ends = jnp.cumsum(padded)
    starts = ends - padded
    run_row = starts[None, :] + run_end - run_len
    run_loc = jnp.cumsum(run_len, axis=1) - run_len
    run_total = jnp.sum(run_len, axis=1, keepdims=True)
    run_info = jnp.concatenate([run_row, run_len, run_loc, run_total], axis=1).reshape(-1).astype(jnp.int32)
    local = (run_loc - block_begin)[:, None, :] + csum.reshape(nb, npair, ne) - 1
    q2 = jnp.sum(local * onehot.reshape(nb, npair, ne), axis=2).reshape(t, 2).astype(jnp.int32)
    nt = (2 * t + nb * ne * al) // tm + ne
    tile_ids = jnp.arange(nt, dtype=jnp.int32)
    tile_valid = (tile_ids * tm < ends[-1]).astype(jnp.int32)
    tile_expert = jnp.minimum(jnp.sum((tile_ids[:, None] * tm >= ends[None, :]).astype(jnp.int32), axis=1), ne - 1)
    tile_src = tile_ids * tile_valid
    pad_info = jnp.concatenate([starts + counts, padded - counts, ends[-1:] // tm]).astype(jnp.int32)
    xs = _dispatch(x2, norm_w, q2, run_info, pad_info, nt * tm)
    y_rows = _experts(xs, tile_expert.astype(jnp.int32), tile_valid, tile_src, w_gate, w_up, w_down)
    return _combine(x2, final_w, q2, run_info, gate_pad, y_rows)


def kernel(x, mix_norm_w, ffn_norm_w, final_norm_w, hg_w_in, hg_lb_logits, hg_norm_w, hg_w_out, ssd_w_in, ssd_conv_w, ssd_conv_b, ssd_dt_bias, ssd_a_log, ssd_d, ssd_norm_w, ssd_w_out, ffn_w_gate, ffn_w_up, ffn_w_down, moe_router, moe_w_gate, moe_w_up, moe_w_down):
    bsz, seq, d = x.shape
    assert mix_norm_w.shape[0] == 2 and hg_w_in.shape[0] == 1 and ssd_w_in.shape[0] == 1
    x = _hgrn_layer(x, mix_norm_w[0], hg_w_in[0], hg_lb_logits, hg_norm_w[0], hg_w_out[0])
    x2 = _ffn_layer(x.reshape(bsz * seq, d), ffn_norm_w[0], ffn_w_gate[0], ffn_w_up[0], ffn_w_down[0])
    x = _ssd_layer(x2.reshape(bsz, seq, d), mix_norm_w[1], ssd_w_in[0], ssd_conv_w[0], ssd_conv_b[0],
                   ssd_dt_bias[0], ssd_a_log[0], ssd_d[0], ssd_norm_w[0], ssd_w_out[0])
    out = _moe_layer_and_final_norm(x.reshape(bsz * seq, d), ffn_norm_w[1], final_norm_w, moe_router[0],
                                    moe_w_gate[0], moe_w_up[0], moe_w_down[0])
    return out.reshape(bsz, seq, d)
```

```python
import functools

import jax
import jax.numpy as jnp
from jax import lax
from jax.experimental import pallas as pl
from jax.experimental.pallas import tpu as pltpu

F32 = jnp.float32
BF16 = jnp.bfloat16
EPS = 1e-6
LOG2_E = 1.4426950408889634
LANES = 128
VMEM_LIMIT = 60000 * 1024

HG_CHUNK = 256
HG_LEVELS = (128, 64, 32, 16, 8, 4, 2, 1)
HG_TOKENS = 256
SSD_CHUNK = 128
SSD_BLOCK = 256
SSD_PROJ_SLAB = 256
SSD_HEADDIM = 64
SSD_STATE = 128
SSD_GROUPS = 4
CONV_W = 4
CONV_TAIL = 8
FFN_TOKENS = 1024
FFN_CHUNK = 2816
MOE_EXPERTS = 8
MOE_TILE = 512
ROUTER_TOKENS = 512
COMBINE_TOKENS = 256
MOE_RUN_ALIGN = 16
MOE_RUN_PIECES = tuple(1 << k for k in range(9, 3, -1))
MOE_RUN_CHUNK = 32
MOE_STAGE_ROWS = 640
MOE_RUN_FIELDS = 3 * MOE_EXPERTS + 1


def _dot(a, b):
    return jnp.dot(a, b, preferred_element_type=F32)


def _dot_nt(a, b):
    return lax.dot_general(a, b, (((1,), (1,)), ((), ())), preferred_element_type=F32)


def _dot_tn(a, b):
    return lax.dot_general(a, b, (((0,), (0,)), ((), ())), preferred_element_type=F32)


def _sigmoid(x):
    return 0.5 * jnp.tanh(0.5 * x) + 0.5


def _rms_rows(x, w):
    return x * lax.rsqrt(jnp.mean(x * x, axis=-1, keepdims=True) + EPS) * w


def _tri_ones(n):
    r = lax.broadcasted_iota(jnp.int32, (n, n), 0)
    c = lax.broadcasted_iota(jnp.int32, (n, n), 1)
    return jnp.where(r >= c, 1.0, 0.0).astype(BF16)


def _cumsum_rows(tri, x):
    hi = x.astype(BF16)
    r1 = x - hi.astype(F32)
    mid = r1.astype(BF16)
    lo = (r1 - mid.astype(F32)).astype(BF16)
    return _dot(tri, hi) + _dot(tri, mid) + _dot(tri, lo)


def _hgrn_kernel(x_ref, mw_ref, win_ref, lbl_ref, nw_ref, wout_ref, o_ref, proj_s, st_s, *, nh, dk, dv):
    tc = x_ref.shape[1]
    qk = nh * dk
    vd = nh * dv

    @pl.when(pl.program_id(1) == 0)
    def _():
        st_s[...] = jnp.zeros_like(st_s)

    x = x_ref[0]
    h = _rms_rows(x, mw_ref[...]).astype(BF16)
    for r0 in range(0, tc, HG_CHUNK):
        proj_s[r0:r0 + HG_CHUNK, :] = _dot(h[r0:r0 + HG_CHUNK], win_ref[...])

    lbl = lbl_ref[...]
    lmax = jnp.max(lbl, axis=0, keepdims=True)
    le = jnp.exp(lbl - lmax)
    lb = le[0:1, :] / jnp.sum(le, axis=0, keepdims=True)

    tri = _tri_ones(HG_CHUNK)
    half = HG_CHUNK // 2
    assert HG_LEVELS[0] == half
    ti = lax.broadcasted_iota(jnp.int32, (half, half), 0)
    si = lax.broadcasted_iota(jnp.int32, (half, half), 1)
    level = jnp.where(si > ti, -2, 31 - lax.clz(ti ^ si))
    level_ids = [hs.bit_length() - 1 for hs in HG_LEVELS[1:]] + [-1]
    sub8 = lax.broadcasted_iota(jnp.int32, (HG_CHUNK // 8, 8, qk), 1)

    def level_decay(b, hs):
        if hs >= 8:
            parts = []
            for lo in range(0, HG_CHUNK, 2 * hs):
                r = b[lo + hs - 1:lo + hs, :]
                parts += [r - b[lo:lo + hs], b[lo + hs:lo + 2 * hs] - r]
            return jnp.exp2(jnp.concatenate(parts, axis=0)).astype(BF16)
        b3 = b.reshape(HG_CHUNK // 8, 8, qk)
        ref = jnp.broadcast_to(b3[:, hs - 1:hs, :], b3.shape)
        for lo in range(2 * hs, 8, 2 * hs):
            ref = jnp.where(sub8 >= lo, jnp.broadcast_to(b3[:, lo + hs - 1:lo + hs, :], b3.shape), ref)
        return jnp.exp2(-jnp.abs(b - ref.reshape(HG_CHUNK, qk))).astype(BF16)

    def chunk_body(c):
        r0 = c * HG_CHUNK
        q = proj_s[pl.ds(r0, HG_CHUNK), 0:qk]
        fp = proj_s[pl.ds(r0, HG_CHUNK), qk:2 * qk]
        v = proj_s[pl.ds(r0, HG_CHUNK), 2 * qk:2 * qk + vd]
        f = lb + (1.0 - lb) * _sigmoid(fp)
        k = 1.0 - f
        b = _cumsum_rows(tri, jnp.log2(f))
        b_last = b[HG_CHUNK - 1:HG_CHUNK, :]
        qg = (q * jnp.exp2(b)).astype(BF16)
        kdec = (k * jnp.exp2(b_last - b)).astype(BF16)
        g_last = jnp.exp2(b_last)
        vb = v.astype(BF16)

        qb = q.astype(BF16)
        kb = k.astype(BF16)
        z_top = level_decay(b, half)
        qd_top = qb[half:] * z_top[half:]
        kd_top = kb[:half] * z_top[:half]
        qds = []
        kds = []
        for hs in HG_LEVELS[1:]:
            z = level_decay(b, hs)
            qds.append(qb * z)
            kds.append(kb * z)
        qds.append(qb)
        kds.append(kb)

        def diagonal_block(rows, ks):
            attn = jnp.zeros((half, half), F32)
            for qd, kd, lid in zip(qds, kds, level_ids):
                attn = jnp.where(level == lid, _dot_nt(qd[rows, ks], kd[rows, ks]), attn)
            return attn

        o_heads = []
        for hh in range(nh):
            ks = slice(hh * dk, (hh + 1) * dk)
            vh = vb[:, hh * dv:(hh + 1) * dv]
            attn_lo = diagonal_block(slice(0, half), ks)
            attn_hi = jnp.concatenate([_dot_nt(qd_top[:, ks], kd_top[:, ks]), diagonal_block(slice(half, HG_CHUNK), ks)],
                                      axis=1)
            o_intra = jnp.concatenate([_dot(attn_lo.astype(BF16), vh[:half]), _dot(attn_hi.astype(BF16), vh)], axis=0)
            st = st_s[hh]
            o_inter = _dot_nt(qg[:, ks], st.astype(BF16))
            oh = o_inter + o_intra
            o_heads.append(oh * lax.rsqrt(jnp.mean(oh * oh, axis=-1, keepdims=True) + EPS))
            st_s[hh] = st * g_last[:, ks] + _dot_tn(vb[:, hh * dv:(hh + 1) * dv], kdec[:, ks])
        g = proj_s[pl.ds(r0, HG_CHUNK), 2 * qk + vd:2 * qk + 2 * vd]
        on = jnp.concatenate(o_heads, axis=1) * nw_ref[...] * _sigmoid(g)
        o_ref[0, pl.ds(r0, HG_CHUNK), :] = x[r0:r0 + HG_CHUNK] + _dot(on.astype(BF16), wout_ref[...])

    for c in range(tc // HG_CHUNK):
        chunk_body(c)


def _hgrn_layer(x, mix_w, w_in, lb_logits, norm_w, w_out):
    bsz, seq, d = x.shape
    dv = norm_w.shape[0]
    nh = w_out.shape[0] // dv
    dk = (w_in.shape[1] - 2 * nh * dv) // (2 * nh)
    assert lb_logits.shape[0] == 2 and dk == LANES and dv == LANES
    tc = HG_TOKENS
    assert seq % tc == 0 and tc % HG_CHUNK == 0
    const = lambda b, j: (0, 0)
    return pl.pallas_call(
        functools.partial(_hgrn_kernel, nh=nh, dk=dk, dv=dv),
        grid=(bsz, seq // tc),
        in_specs=[
            pl.BlockSpec((1, tc, d), lambda b, j: (b, j, 0)),
            pl.BlockSpec((1, d), const),
            pl.BlockSpec(w_in.shape, const, pipeline_mode=pl.Buffered(1)),
            pl.BlockSpec(lb_logits.shape, const),
            pl.BlockSpec((1, nh * dv), const),
            pl.BlockSpec(w_out.shape, const, pipeline_mode=pl.Buffered(1)),
        ],
        out_specs=pl.BlockSpec((1, tc, d), lambda b, j: (b, j, 0)),
        out_shape=jax.ShapeDtypeStruct(x.shape, F32),
        scratch_shapes=[
            pltpu.VMEM((tc, w_in.shape[1]), F32),
            pltpu.VMEM((nh, dv, dk), F32),
        ],
        compiler_params=pltpu.CompilerParams(
            dimension_semantics=("arbitrary", "arbitrary"), vmem_limit_bytes=VMEM_LIMIT),
        name="hgrn_mixer",
    )(x, mix_w.reshape(1, d), w_in.astype(BF16), lb_logits, jnp.tile(norm_w, nh).reshape(1, nh * dv),
      w_out.astype(BF16))


def _swiglu_rows(h, wg_ref, wu_ref, wd_ref, acc):
    dff = wg_ref.shape[-1]
    for c0 in range(0, dff, FFN_CHUNK):
        g = _dot(h, wg_ref[:, c0:c0 + FFN_CHUNK])
        u = _dot(h, wu_ref[:, c0:c0 + FFN_CHUNK])
        a = (g * _sigmoid(g) * u).astype(BF16)
        y = _dot(a, wd_ref[c0:c0 + FFN_CHUNK, :])
        acc = y if acc is None else acc + y
    return acc


def _ffn_kernel(x_ref, nw_ref, wg_ref, wu_ref, wd_ref, o_ref):
    x = x_ref[...]
    h = _rms_rows(x, nw_ref[...]).astype(BF16)
    o_ref[...] = _swiglu_rows(h, wg_ref, wu_ref, wd_ref, x)


def _ffn_layer(x2, norm_w, w_gate, w_up, w_down):
    t, d = x2.shape
    dff = w_gate.shape[1]
    assert t % FFN_TOKENS == 0 and dff % FFN_CHUNK == 0
    const = lambda i: (0, 0)
    return pl.pallas_call(
        _ffn_kernel,
        grid=(t // FFN_TOKENS,),
        in_specs=[
            pl.BlockSpec((FFN_TOKENS, d), lambda i: (i, 0)),
            pl.BlockSpec((1, d), const),
            pl.BlockSpec((d, dff), const, pipeline_mode=pl.Buffered(1)),
            pl.BlockSpec((d, dff), const, pipeline_mode=pl.Buffered(1)),
            pl.BlockSpec((dff, d), const, pipeline_mode=pl.Buffered(1)),
        ],
        out_specs=pl.BlockSpec((FFN_TOKENS, d), lambda i: (i, 0)),
        out_shape=jax.ShapeDtypeStruct(x2.shape, F32),
        compiler_params=pltpu.CompilerParams(
            dimension_semantics=("arbitrary",), vmem_limit_bytes=VMEM_LIMIT),
        name="dense_ffn",
    )(x2, norm_w.reshape(1, d), w_gate.astype(BF16), w_up.astype(BF16), w_down.astype(BF16))


def _ssd_mix_block(proj_s, r0, side, x_ref, cw_ref, cb_ref, dtb_ref, alog_ref, dsk_ref, nw_ref, wout_ref, o_ref,
                   cbuf, y_s, st_s, *, d_inner):
    tc = SSD_BLOCK
    p2 = 2 * SSD_HEADDIM
    assert p2 == LANES and SSD_STATE == LANES
    gs = SSD_GROUPS * SSD_STATE
    cd = d_inner + 2 * gs
    pairs_per_group = d_inner // (SSD_GROUPS * p2)
    q = SSD_CHUNK
    assert q == LANES
    n_points = cd // LANES + (tc // q) * SSD_GROUPS * pairs_per_group
    emitted = [0, 0]

    def interleave():
        emitted[0] += 1
        while emitted[1] < len(side) and emitted[1] * n_points < emitted[0] * len(side):
            side[emitted[1]]()
            emitted[1] += 1

    for j in range(cd // LANES):
        c0 = d_inner + j * LANES
        cbuf[j, CONV_TAIL:CONV_TAIL + tc, :] = proj_s[:, c0:c0 + LANES]
        conv = cb_ref[:, j * LANES:(j + 1) * LANES]
        for k in range(CONV_W):
            off = CONV_TAIL - (CONV_W - 1) + k
            conv = conv + cw_ref[k:k + 1, j * LANES:(j + 1) * LANES] * cbuf[j, off:off + tc, :]
        cbuf[j, 0:CONV_TAIL, :] = cbuf[j, tc:tc + CONV_TAIL, :]
        proj_s[:, c0:c0 + LANES] = conv * _sigmoid(conv)
        interleave()

    dt_pre = proj_s[:, d_inner + cd:d_inner + cd + LANES] + dtb_ref[...]
    dt = jnp.maximum(dt_pre, 0.0) + jnp.log1p(jnp.exp(-jnp.abs(dt_pre)))
    da = dt * (-LOG2_E * jnp.exp(alog_ref[...]))

    tri = _tri_ones(q)
    rr = lax.broadcasted_iota(jnp.int32, (q, q), 0)
    cc = lax.broadcasted_iota(jnp.int32, (q, q), 1)
    causal = rr >= cc
    first = lax.broadcasted_iota(jnp.int32, (1, LANES), 1) < SSD_HEADDIM

    def lane_col(a, hd):
        return jnp.broadcast_to(a[:, hd:hd + 1], (q, LANES))

    for sc in range(tc // q):
        lo, hi = sc * q, (sc + 1) * q
        dtc = dt[lo:hi]
        cs = _cumsum_rows(tri, da[lo:hi])
        last = cs[q - 1:q, :]
        cs_t = cs.T
        for g in range(SSD_GROUPS):
            bm = proj_s[lo:hi, 2 * d_inner + g * SSD_STATE:2 * d_inner + (g + 1) * SSD_STATE]
            cm = proj_s[lo:hi, 2 * d_inner + gs + g * SSD_STATE:2 * d_inner + gs + (g + 1) * SSD_STATE]
            bmb = bm.astype(BF16)
            cmb = cm.astype(BF16)
            cb = _dot_nt(cmb, bmb)
            for pr in range(pairs_per_group):
                pi = g * pairs_per_group + pr
                h1, h2 = 2 * pi, 2 * pi + 1
                xp = proj_s[lo:hi, d_inner + pi * p2:d_inner + (pi + 1) * p2]
                c1 = lane_col(cs, h1)
                c2 = lane_col(cs, h2)
                cs_p = jnp.where(first, c1, c2)
                last_p = jnp.where(first, last[:, h1:h1 + 1], last[:, h2:h2 + 1])
                xdt = jnp.where(first, lane_col(dtc, h1), lane_col(dtc, h2)) * xp
                m1 = cb * jnp.exp2(jnp.where(causal, c1 - cs_t[h1:h1 + 1, :], -jnp.inf))
                m2 = cb * jnp.exp2(jnp.where(causal, c2 - cs_t[h2:h2 + 1, :], -jnp.inf))
                x1 = jnp.where(first, xdt, 0.0).astype(BF16)
                x2 = jnp.where(first, 0.0, xdt).astype(BF16)
                y = _dot(m1.astype(BF16), x1) + _dot(m2.astype(BF16), x2)
                st = st_s[pi]
                y = y + jnp.exp2(cs_p) * _dot(cmb, st.astype(BF16))
                wx = (jnp.exp2(last_p - cs_p) * xdt).astype(BF16)
                st_s[pi] = st * jnp.exp2(last_p) + _dot_tn(bmb, wx)
                y_s[lo:hi, pi * p2:(pi + 1) * p2] = y + dsk_ref[:, pi * p2:(pi + 1) * p2] * xp
                interleave()

    z = proj_s[:, 0:d_inner]
    y = y_s[...] * (z * _sigmoid(z))
    gw = d_inner // SSD_GROUPS
    parts = []
    for g in range(SSD_GROUPS):
        yg = y[:, g * gw:(g + 1) * gw]
        parts.append(yg * lax.rsqrt(jnp.mean(yg * yg, axis=-1, keepdims=True) + EPS))
    yn = jnp.concatenate(parts, axis=1) * nw_ref[...]
    o_ref[r0:r0 + tc, :] = x_ref[r0:r0 + tc, :] + _dot(yn.astype(BF16), wout_ref[...])


def _ssd_kernel(x_ref, xn_ref, mw_ref, win_ref, cw_ref, cb_ref, dtb_ref, alog_ref, dsk_ref, nw_ref, wout_ref, o_ref,
                proj_a, proj_b, cbuf, y_s, st_s, *, d_inner, steps_per_seq):
    tb = SSD_BLOCK
    n_all = win_ref.shape[1]
    step = pl.program_id(0)

    @pl.when(lax.rem(step, steps_per_seq) == 0)
    def _():
        st_s[...] = jnp.zeros_like(st_s)
        cbuf[:, 0:CONV_TAIL, :] = jnp.zeros((cbuf.shape[0], CONV_TAIL, LANES), F32)

    def projection_pieces(rows_ref, r0, dst):
        cache = []

        def piece(c0, c1):
            if not cache:
                cache.append(_rms_rows(rows_ref[r0:r0 + tb, :], mw_ref[...]).astype(BF16))
            dst[:, c0:c1] = _dot(cache[0], win_ref[:, c0:c1])

        return [functools.partial(piece, c0, min(c0 + SSD_PROJ_SLAB, n_all)) for c0 in range(0, n_all, SSD_PROJ_SLAB)]

    @pl.when(step == 0)
    def _():
        for piece in projection_pieces(x_ref, 0, proj_a):
            piece()

    mix = functools.partial(_ssd_mix_block, x_ref=x_ref, cw_ref=cw_ref, cb_ref=cb_ref, dtb_ref=dtb_ref,
                            alog_ref=alog_ref, dsk_ref=dsk_ref, nw_ref=nw_ref, wout_ref=wout_ref, o_ref=o_ref,
                            cbuf=cbuf, y_s=y_s, st_s=st_s, d_inner=d_inner)
    mix(proj_a, 0, projection_pieces(x_ref, tb, proj_b))
    mix(proj_b, tb, projection_pieces(xn_ref, 0, proj_a))


def _ssd_layer(x, mix_w, w_in, conv_w, conv_b, dt_bias, a_log, d_skip, norm_w, w_out):
    bsz, seq, d = x.shape
    d_inner = w_out.shape[0]
    nheads = dt_bias.shape[0]
    assert d_inner == nheads * SSD_HEADDIM and nheads <= LANES
    cd = d_inner + 2 * SSD_GROUPS * SSD_STATE
    assert conv_w.shape == (CONV_W, cd) and w_in.shape[1] == d_inner + cd + nheads
    tb = SSD_BLOCK
    assert seq % (2 * tb) == 0 and tb % SSD_CHUNK == 0
    t = bsz * seq
    pad = LANES - nheads
    w_all = jnp.pad(w_in, ((0, 0), (0, pad))).astype(BF16)
    n_all = w_all.shape[1]
    const = lambda i: (0, 0)
    last_block = t // tb - 1
    x2 = x.reshape(t, d)
    out = pl.pallas_call(
        functools.partial(_ssd_kernel, d_inner=d_inner, steps_per_seq=seq // (2 * tb)),
        grid=(t // (2 * tb),),
        in_specs=[
            pl.BlockSpec((2 * tb, d), lambda i: (i, 0)),
            pl.BlockSpec((tb, d), lambda i: (jnp.minimum(2 * i + 2, last_block), 0)),
            pl.BlockSpec((1, d), const),
            pl.BlockSpec(w_all.shape, const, pipeline_mode=pl.Buffered(1)),
            pl.BlockSpec((CONV_W, cd), const),
            pl.BlockSpec((1, cd), const),
            pl.BlockSpec((1, LANES), const),
            pl.BlockSpec((1, LANES), const),
            pl.BlockSpec((1, d_inner), const),
            pl.BlockSpec((1, d_inner), const),
            pl.BlockSpec(w_out.shape, const, pipeline_mode=pl.Buffered(1)),
        ],
        out_specs=pl.BlockSpec((2 * tb, d), lambda i: (i, 0)),
        out_shape=jax.ShapeDtypeStruct((t, d), F32),
        scratch_shapes=[
            pltpu.VMEM((tb, n_all), F32),
            pltpu.VMEM((tb, n_all), F32),
            pltpu.VMEM((cd // LANES, tb + CONV_TAIL, LANES), F32),
            pltpu.VMEM((tb, d_inner), F32),
            pltpu.VMEM((nheads // 2, SSD_STATE, LANES), F32),
        ],
        compiler_params=pltpu.CompilerParams(
            dimension_semantics=("arbitrary",), vmem_limit_bytes=VMEM_LIMIT),
        name="ssd_mixer",
    )(x2, x2, mix_w.reshape(1, d), w_all, conv_w, conv_b.reshape(1, cd),
      jnp.pad(dt_bias, (0, pad)).reshape(1, LANES), jnp.pad(a_log, (0, pad)).reshape(1, LANES),
      jnp.repeat(d_skip, SSD_HEADDIM).reshape(1, d_inner), norm_w.reshape(1, d_inner), w_out.astype(BF16))
    return out.reshape(bsz, seq, d)


def _router_kernel(x_ref, nw_ref, rw_ref, idx_ref, gate_ref):
    h = _rms_rows(x_ref[...], nw_ref[...])
    hi = h.astype(BF16)
    lo = (h - hi.astype(F32)).astype(BF16)
    both = _dot(hi, rw_ref[...])
    logits = both[:, :LANES] + both[:, LANES:] + _dot(lo, rw_ref[:, :LANES])
    lane = lax.broadcasted_iota(jnp.int32, logits.shape, 1)
    valid = lane < MOE_EXPERTS
    lg = jnp.where(valid, logits, -jnp.inf)
    e = jnp.exp(lg - jnp.max(lg, axis=-1, keepdims=True))
    p = jnp.where(valid, e / jnp.sum(e, axis=-1, keepdims=True), -1.0)
    m1 = jnp.max(p, axis=-1, keepdims=True)
    i1 = jnp.min(jnp.where(p == m1, lane, LANES), axis=-1, keepdims=True)
    p2 = jnp.where(lane == i1, -1.0, p)
    m2 = jnp.max(p2, axis=-1, keepdims=True)
    i2 = jnp.min(jnp.where(p2 == m2, lane, LANES), axis=-1, keepdims=True)
    s = m1 + m2
    idx_ref[...] = jnp.where(lane == 0, i1, jnp.where(lane == 1, i2, 0))
    gate_ref[...] = jnp.where(lane == 0, m1 / s, jnp.where(lane == 1, m2 / s, 0.0))


def _router(x2, norm_w, router_w):
    t, d = x2.shape
    ne = router_w.shape[1]
    assert ne == MOE_EXPERTS and t % ROUTER_TOKENS == 0
    rw = jnp.pad(router_w, ((0, 0), (0, LANES - ne)))
    rw_hi = rw.astype(BF16)
    rw_lo = (rw - rw_hi.astype(F32)).astype(BF16)
    rw = jnp.concatenate([rw_hi, rw_lo], axis=1)
    return pl.pallas_call(
        _router_kernel,
        grid=(t // ROUTER_TOKENS,),
        in_specs=[
            pl.BlockSpec((ROUTER_TOKENS, d), lambda i: (i, 0)),
            pl.BlockSpec((1, d), lambda i: (0, 0)),
            pl.BlockSpec((d, 2 * LANES), lambda i: (0, 0)),
        ],
        out_specs=[pl.BlockSpec((ROUTER_TOKENS, LANES), lambda i: (i, 0))] * 2,
        out_shape=[jax.ShapeDtypeStruct((t, LANES), jnp.int32), jax.ShapeDtypeStruct((t, LANES), F32)],
        compiler_params=pltpu.CompilerParams(dimension_semantics=("arbitrary",)),
        name="moe_router",
    )(x2, norm_w.reshape(1, d), rw)


def _for_binary_pieces(length, sizes, piece_fn, off=0):
    for size in sizes:
        take = (length & size) != 0

        @pl.when(take)
        def _(off=off, size=size):
            piece_fn(off, size)

        off = off + jnp.where(take, size, 0)


def _for_run_pieces(length, piece_fn):
    n_chunks = length // MOE_RUN_CHUNK

    def chunk(c, carry):
        piece_fn(c * MOE_RUN_CHUNK, MOE_RUN_CHUNK)
        return carry

    lax.fori_loop(0, n_chunks, chunk, 0)
    _for_binary_pieces(length, [s for s in MOE_RUN_PIECES if s < MOE_RUN_CHUNK], piece_fn, off=n_chunks * MOE_RUN_CHUNK)


def _aligned(x):
    return pl.multiple_of(x, MOE_RUN_ALIGN)


def _pair_one_hot(q_col, n):
    lane = lax.broadcasted_iota(jnp.int32, (q_col.shape[0], n), 1)
    return jnp.where(lane == q_col, 1.0, 0.0)


def _dispatch_kernel(run_ref, pad_ref, x_ref, q_ref, nw_ref, xs_hbm, stage, zbuf, sem, pad_sem):
    i = pl.program_id(0)
    last = pl.num_programs(0) - 1
    npair = stage.shape[0]
    ne = MOE_EXPERTS

    h = _rms_rows(x_ref[...], nw_ref[...]).astype(BF16)
    q = q_ref[...]
    sel = (_pair_one_hot(q[:, 0:1], npair) + _pair_one_hot(q[:, 1:2], npair)).astype(BF16)
    ordered = _dot_tn(sel, h)

    def start_block(blk):
        base = blk * MOE_RUN_FIELDS
        for e in range(ne):
            dst0 = run_ref[base + e]
            loc0 = run_ref[base + 2 * ne + e]

            def piece(off, size, dst0=dst0, loc0=loc0):
                pltpu.make_async_copy(stage.at[pl.ds(_aligned(loc0 + off), size), :],
                                      xs_hbm.at[pl.ds(_aligned(dst0 + off), size), :], sem).start()

            _for_run_pieces(run_ref[base + ne + e], piece)

    def wait_block(blk):
        def piece(off, size):
            pltpu.make_async_copy(stage.at[pl.ds(0, size), :], xs_hbm.at[pl.ds(0, size), :], sem).wait()

        _for_binary_pieces(run_ref[blk * MOE_RUN_FIELDS + 3 * ne], MOE_RUN_PIECES, piece)

    @pl.when(i > 0)
    def _():
        wait_block(i - 1)

    stage[...] = ordered.astype(BF16)
    start_block(i)

    @pl.when(i == last)
    def _():
        wait_block(i)
        zbuf[...] = jnp.zeros_like(zbuf)
        for e in range(ne):
            first = pad_ref[e]
            count = pad_ref[ne + e]

            def zero_copy(r, first=first):
                return pltpu.make_async_copy(zbuf.at[pl.ds(0, MOE_RUN_ALIGN), :],
                                             xs_hbm.at[pl.ds(_aligned(first + r * MOE_RUN_ALIGN), MOE_RUN_ALIGN), :],
                                             pad_sem)

            def zero_issue(r, carry):
                zero_copy(r).start()
                return carry

            def zero_wait(r, carry):
                zero_copy(r).wait()
                return carry

            lax.fori_loop(0, count // MOE_RUN_ALIGN, zero_issue, 0)
            lax.fori_loop(0, count // MOE_RUN_ALIGN, zero_wait, 0)

        tm = zbuf.shape[0]
        first_tile = pad_ref[2 * ne]

        def tile_copy(j):
            return pltpu.make_async_copy(zbuf, xs_hbm.at[pl.ds((first_tile + j) * tm, tm), :], pad_sem)

        def tile_issue(j, carry):
            tile_copy(j).start()
            return carry

        def tile_wait(j, carry):
            tile_copy(j).wait()
            return carry

        n_unused = xs_hbm.shape[0] // tm - first_tile
        lax.fori_loop(0, n_unused, tile_issue, 0)
        lax.fori_loop(0, n_unused, tile_wait, 0)


def _dispatch(x2, norm_w, q2, run_info, pad_info, n_rows):
    t, d = x2.shape
    tb = COMBINE_TOKENS
    assert t % tb == 0
    grid_spec = pltpu.PrefetchScalarGridSpec(
        num_scalar_prefetch=2,
        grid=(t // tb,),
        in_specs=[
            pl.BlockSpec((tb, d), lambda i, run, pad: (i, 0)),
            pl.BlockSpec((tb, 2), lambda i, run, pad: (i, 0)),
            pl.BlockSpec((1, d), lambda i, run, pad: (0, 0)),
        ],
        out_specs=pl.BlockSpec(memory_space=pl.ANY),
        scratch_shapes=[pltpu.VMEM((MOE_STAGE_ROWS, d), BF16), pltpu.VMEM((MOE_TILE, d), BF16),
                        pltpu.SemaphoreType.DMA(()), pltpu.SemaphoreType.DMA(())],
    )
    return pl.pallas_call(
        _dispatch_kernel,
        grid_spec=grid_spec,
        out_shape=jax.ShapeDtypeStruct((n_rows, d), BF16),
        compiler_params=pltpu.CompilerParams(dimension_semantics=("arbitrary",)),
        name="moe_dispatch",
    )(run_info, pad_info, x2, q2, norm_w.reshape(1, d))


def _experts_kernel(te_ref, tv_ref, ts_ref, xs_ref, wg_ref, wu_ref, wd_ref, y_ref):
    i = pl.program_id(0)

    @pl.when(tv_ref[i] > 0)
    def _():
        y_ref[...] = _swiglu_rows(xs_ref[...], wg_ref.at[0], wu_ref.at[0], wd_ref.at[0], None).astype(BF16)

    @pl.when(tv_ref[i] == 0)
    def _():
        y_ref[...] = jnp.zeros_like(y_ref)


def _experts(xs, tile_expert, tile_valid, tile_src, w_gate, w_up, w_down):
    n_rows, d = xs.shape
    ne, _, dff = w_gate.shape
    tm = MOE_TILE
    nt = n_rows // tm
    wmap = lambda i, te, tv, ts: (te[i], 0, 0)
    grid_spec = pltpu.PrefetchScalarGridSpec(
        num_scalar_prefetch=3,
        grid=(nt,),
        in_specs=[
            pl.BlockSpec((tm, d), lambda i, te, tv, ts: (ts[i], 0)),
            pl.BlockSpec((1, d, dff), wmap),
            pl.BlockSpec((1, d, dff), wmap),
            pl.BlockSpec((1, dff, d), wmap),
        ],
        out_specs=pl.BlockSpec((tm, d), lambda i, te, tv, ts: (i, 0)),
    )
    return pl.pallas_call(
        _experts_kernel,
        grid_spec=grid_spec,
        out_shape=jax.ShapeDtypeStruct((n_rows, d), BF16),
        compiler_params=pltpu.CompilerParams(
            dimension_semantics=("arbitrary",), vmem_limit_bytes=VMEM_LIMIT),
        name="moe_experts",
    )(tile_expert, tile_valid, tile_src, xs, w_gate.astype(BF16), w_up.astype(BF16), w_down.astype(BF16))


def _combine_kernel(run_ref, x_ref, q_ref, gate_ref, fw_ref, y_hbm, o_ref, ybuf, sem):
    i = pl.program_id(0)
    npair = ybuf.shape[1]
    ne = MOE_EXPERTS
    slot = lax.rem(i, 2)

    def start_block(blk, s):
        tail = 2 * x_ref.shape[0]
        ybuf[s, tail:, :] = jnp.zeros((npair - tail, ybuf.shape[2]), BF16)
        base = blk * MOE_RUN_FIELDS
        for e in range(ne):
            src0 = run_ref[base + e]
            loc0 = run_ref[base + 2 * ne + e]

            def piece(off, size, src0=src0, loc0=loc0):
                pltpu.make_async_copy(y_hbm.at[pl.ds(_aligned(src0 + off), size), :],
                                      ybuf.at[s, pl.ds(_aligned(loc0 + off), size), :], sem.at[s]).start()

            _for_run_pieces(run_ref[base + ne + e], piece)

    def wait_block(blk, s):
        def piece(off, size):
            pltpu.make_async_copy(y_hbm.at[pl.ds(0, size), :], ybuf.at[s, pl.ds(0, size), :], sem.at[s]).wait()

        _for_binary_pieces(run_ref[blk * MOE_RUN_FIELDS + 3 * ne], MOE_RUN_PIECES, piece)

    @pl.when(i == 0)
    def _():
        start_block(0, 0)

    @pl.when(i + 1 < pl.num_programs(0))
    def _():
        start_block(i + 1, 1 - slot)

    wait_block(i, slot)
    yb = ybuf[slot]
    q = q_ref[...]
    g = gate_ref[...]
    y0 = _dot(_pair_one_hot(q[:, 0:1], npair).astype(BF16), yb)
    y1 = _dot(_pair_one_hot(q[:, 1:2], npair).astype(BF16), yb)
    o = x_ref[...] + g[:, 0:1] * y0 + g[:, 1:2] * y1
    o_ref[...] = _rms_rows(o, fw_ref[...])


def _combine(x2, final_w, q2, run_info, gate_pad, y_rows):
    t, d = x2.shape
    tb = COMBINE_TOKENS
    nb = t // tb
    grid_spec = pltpu.PrefetchScalarGridSpec(
        num_scalar_prefetch=1,
        grid=(nb,),
        in_specs=[
            pl.BlockSpec((tb, d), lambda i, run: (i, 0)),
            pl.BlockSpec((tb, 2), lambda i, run: (i, 0)),
            pl.BlockSpec((tb, LANES), lambda i, run: (i, 0)),
            pl.BlockSpec((1, d), lambda i, run: (0, 0)),
            pl.BlockSpec(memory_space=pl.ANY),
        ],
        out_specs=pl.BlockSpec((tb, d), lambda i, run: (i, 0)),
        scratch_shapes=[pltpu.VMEM((2, MOE_STAGE_ROWS, d), BF16), pltpu.SemaphoreType.DMA((2,))],
    )
    return pl.pallas_call(
        _combine_kernel,
        grid_spec=grid_spec,
        out_shape=jax.ShapeDtypeStruct(x2.shape, F32),
        compiler_params=pltpu.CompilerParams(dimension_semantics=("arbitrary",)),
        name="moe_combine",
    )(run_info, x2, q2, gate_pad, final_w.reshape(1, d), y_rows)


def _moe_layer_and_final_norm(x2, norm_w, final_w, router_w, w_gate, w_up, w_down):
    t, d = x2.shape
    ne = MOE_EXPERTS
    tm = MOE_TILE
    tb = COMBINE_TOKENS
    nb = t // tb
    npair = 2 * tb
    idx_pad, gate_pad = _router(x2, norm_w, router_w)
    flat_e = idx_pad[:, :2].reshape(-1)
    onehot = (flat_e[:, None] == jnp.arange(ne, dtype=jnp.int32)[None, :]).astype(jnp.int32)
    csum = jnp.cumsum(onehot, axis=0)
    block_end = csum.reshape(nb, npair, ne)[:, -1, :]
    block_begin = jnp.concatenate([jnp.zeros((1, ne), jnp.int32), block_end[:-1]], axis=0)
    al = MOE_RUN_ALIGN
    run_len = ((block_end - block_begin + al - 1) // al) * al
    run_end = jnp.cumsum(run_len, axis=0)
    counts = run_end[-1]
    padded = ((counts + tm - 1) // tm) * tm
    ends = jnp.cumsum(padded)
    starts = ends - padded
    run_row = starts[None, :] + run_end - run_len
    run_loc = jnp.cumsum(run_len, axis=1) - run_len
    run_total = jnp.sum(run_len, axis=1, keepdims=True)
    run_info = jnp.concatenate([run_row, run_len, run_loc, run_total], axis=1).reshape(-1).astype(jnp.int32)
    local = (run_loc - block_begin)[:, None, :] + csum.reshape(nb, npair, ne) - 1
    q2 = jnp.sum(local * onehot.reshape(nb, npair, ne), axis=2).reshape(t, 2).astype(jnp.int32)
    nt = (2 * t + nb * ne * al) // tm + ne
    tile_ids = jnp.arange(nt, dtype=jnp.int32)
    tile_valid = (tile_ids * tm < ends[-1]).astype(jnp.int32)
    tile_expert = jnp.minimum(jnp.sum((tile_ids[:, None] * tm >= ends[None, :]).astype(jnp.int32), axis=1), ne - 1)
    tile_src = tile_ids * tile_valid
    pad_info = jnp.concatenate([starts + counts, padded - counts, ends[-1:] // tm]).astype(jnp.int32)
    xs = _dispatch(x2, norm_w, q2, run_info, pad_info, nt * tm)
    y_rows = _experts(xs, tile_expert.astype(jnp.int32), tile_valid, tile_src, w_gate, w_up, w_down)
    return _combine(x2, final_w, q2, run_info, gate_pad, y_rows)


def kernel(x, mix_norm_w, ffn_norm_w, final_norm_w, hg_w_in, hg_lb_logits, hg_norm_w, hg_w_out, ssd_w_in, ssd_conv_w, ssd_conv_b, ssd_dt_bias, ssd_a_log, ssd_d, ssd_norm_w, ssd_w_out, ffn_w_gate, ffn_w_up, ffn_w_down, moe_router, moe_w_gate, moe_w_up, moe_w_down):
    bsz, seq, d = x.shape
    assert mix_norm_w.shape[0] == 2 and hg_w_in.shape[0] == 1 and ssd_w_in.shape[0] == 1
    x = _hgrn_layer(x, mix_norm_w[0], hg_w_in[0], hg_lb_logits, hg_norm_w[0], hg_w_out[0])
    x2 = _ffn_layer(x.reshape(bsz * seq, d), ffn_norm_w[0], ffn_w_gate[0], ffn_w_up[0], ffn_w_down[0])
    x = _ssd_layer(x2.reshape(bsz, seq, d), mix_norm_w[1], ssd_w_in[0], ssd_conv_w[0], ssd_conv_b[0],
                   ssd_dt_bias[0], ssd_a_log[0], ssd_d[0], ssd_norm_w[0], ssd_w_out[0])
    out = _moe_layer_and_final_norm(x.reshape(bsz * seq, d), ffn_norm_w[1], final_norm_w, moe_router[0],
                                    moe_w_gate[0], moe_w_up[0], moe_w_down[0])
    return out.reshape(bsz, seq, d)
```

```python
import functools

import jax
import jax.numpy as jnp
from jax import lax
from jax.experimental import pallas as pl
from jax.experimental.pallas import tpu as pltpu

F32 = jnp.float32
BF16 = jnp.bfloat16
EPS = 1e-6
LOG2_E = 1.4426950408889634
LANES = 128
VMEM_LIMIT = 60000 * 1024

HG_CHUNK = 256
HG_LEVELS = (128, 64, 32, 16, 8, 4, 2, 1)
HG_TOKENS = 512
SSD_CHUNK = 128
SSD_BLOCK = 256
SSD_PROJ_SLAB = 256
SSD_HEADDIM = 64
SSD_STATE = 128
SSD_GROUPS = 4
CONV_W = 4
CONV_TAIL = 8
FFN_TOKENS = 1024
FFN_CHUNK = 2816
MOE_EXPERTS = 8
MOE_TILE = 512
ROUTER_TOKENS = 512
COMBINE_TOKENS = 256
MOE_RUN_ALIGN = 8
MOE_RUN_PIECES = tuple(1 << k for k in range(9, 2, -1))
MOE_RUN_CHUNK = 32
MOE_STAGE_ROWS = 640
MOE_RUN_FIELDS = 3 * MOE_EXPERTS + 1


def _dot(a, b):
    return jnp.dot(a, b, preferred_element_type=F32)


def _dot_nt(a, b):
    return lax.dot_general(a, b, (((1,), (1,)), ((), ())), preferred_element_type=F32)


def _dot_tn(a, b):
    return lax.dot_general(a, b, (((0,), (0,)), ((), ())), preferred_element_type=F32)


def _sigmoid(x):
    return 0.5 * jnp.tanh(0.5 * x) + 0.5


def _rms_rows(x, w):
    return x * lax.rsqrt(jnp.mean(x * x, axis=-1, keepdims=True) + EPS) * w


def _tri_ones(n):
    r = lax.broadcasted_iota(jnp.int32, (n, n), 0)
    c = lax.broadcasted_iota(jnp.int32, (n, n), 1)
    return jnp.where(r >= c, 1.0, 0.0).astype(BF16)


def _cumsum_rows(tri, x):
    hi = x.astype(BF16)
    r1 = x - hi.astype(F32)
    mid = r1.astype(BF16)
    lo = (r1 - mid.astype(F32)).astype(BF16)
    return _dot(tri, hi) + _dot(tri, mid) + _dot(tri, lo)


def _hgrn_kernel(x_ref, mw_ref, win_ref, lbl_ref, nw_ref, wout_ref, o_ref, proj_s, st_s, *, nh, dk, dv):
    tc = x_ref.shape[1]
    qk = nh * dk
    vd = nh * dv

    @pl.when(pl.program_id(1) == 0)
    def _():
        st_s[...] = jnp.zeros_like(st_s)

    x = x_ref[0]
    h = _rms_rows(x, mw_ref[...]).astype(BF16)
    for r0 in range(0, tc, HG_CHUNK):
        proj_s[r0:r0 + HG_CHUNK, :] = _dot(h[r0:r0 + HG_CHUNK], win_ref[...])

    lbl = lbl_ref[...]
    lmax = jnp.max(lbl, axis=0, keepdims=True)
    le = jnp.exp(lbl - lmax)
    lb = le[0:1, :] / jnp.sum(le, axis=0, keepdims=True)

    tri = _tri_ones(HG_CHUNK)
    half = HG_CHUNK // 2
    assert HG_LEVELS[0] == half
    ti = lax.broadcasted_iota(jnp.int32, (half, half), 0)
    si = lax.broadcasted_iota(jnp.int32, (half, half), 1)
    level = jnp.where(si > ti, -2, 31 - lax.clz(ti ^ si))
    level_ids = [hs.bit_length() - 1 for hs in HG_LEVELS[1:]] + [-1]
    sub8 = lax.broadcasted_iota(jnp.int32, (HG_CHUNK // 8, 8, qk), 1)

    def level_decay(b, hs):
        if hs >= 8:
            parts = []
            for lo in range(0, HG_CHUNK, 2 * hs):
                r = b[lo + hs - 1:lo + hs, :]
                parts += [r - b[lo:lo + hs], b[lo + hs:lo + 2 * hs] - r]
            return jnp.exp2(jnp.concatenate(parts, axis=0)).astype(BF16)
        b3 = b.reshape(HG_CHUNK // 8, 8, qk)
        ref = jnp.broadcast_to(b3[:, hs - 1:hs, :], b3.shape)
        for lo in range(2 * hs, 8, 2 * hs):
            ref = jnp.where(sub8 >= lo, jnp.broadcast_to(b3[:, lo + hs - 1:lo + hs, :], b3.shape), ref)
        return jnp.exp2(-jnp.abs(b - ref.reshape(HG_CHUNK, qk))).astype(BF16)

    def chunk_body(c):
        r0 = c * HG_CHUNK
        q = proj_s[pl.ds(r0, HG_CHUNK), 0:qk]
        fp = proj_s[pl.ds(r0, HG_CHUNK), qk:2 * qk]
        v = proj_s[pl.ds(r0, HG_CHUNK), 2 * qk:2 * qk + vd]
        f = lb + (1.0 - lb) * _sigmoid(fp)
        k = 1.0 - f
        b = _cumsum_rows(tri, jnp.log2(f))
        b_last = b[HG_CHUNK - 1:HG_CHUNK, :]
        qg = (q * jnp.exp2(b)).astype(BF16)
        kdec = (k * jnp.exp2(b_last - b)).astype(BF16)
        g_last = jnp.exp2(b_last)
        vb = v.astype(BF16)

        qb = q.astype(BF16)
        kb = k.astype(BF16)
        z_top = level_decay(b, half)
        qd_top = qb[half:] * z_top[half:]
        kd_top = kb[:half] * z_top[:half]
        qds = []
        kds = []
        for hs in HG_LEVELS[1:]:
            z = level_decay(b, hs)
            qds.append(qb * z)
            kds.append(kb * z)
        qds.append(qb)
        kds.append(kb)

        def diagonal_block(rows, ks):
            attn = jnp.zeros((half, half), F32)
            for qd, kd, lid in zip(qds, kds, level_ids):
                attn = jnp.where(level == lid, _dot_nt(qd[rows, ks], kd[rows, ks]), attn)
            return attn

        o_heads = []
        for hh in range(nh):
            ks = slice(hh * dk, (hh + 1) * dk)
            vh = vb[:, hh * dv:(hh + 1) * dv]
            attn_lo = diagonal_block(slice(0, half), ks)
            attn_hi = jnp.concatenate([_dot_nt(qd_top[:, ks], kd_top[:, ks]), diagonal_block(slice(half, HG_CHUNK), ks)],
                                      axis=1)
            o_intra = jnp.concatenate([_dot(attn_lo.astype(BF16), vh[:half]), _dot(attn_hi.astype(BF16), vh)], axis=0)
            st = st_s[hh]
            o_inter = _dot_nt(qg[:, ks], st.astype(BF16))
            oh = o_inter + o_intra
            o_heads.append(oh * lax.rsqrt(jnp.mean(oh * oh, axis=-1, keepdims=True) + EPS))
            st_s[hh] = st * g_last[:, ks] + _dot_tn(vb[:, hh * dv:(hh + 1) * dv], kdec[:, ks])
        g = proj_s[pl.ds(r0, HG_CHUNK), 2 * qk + vd:2 * qk + 2 * vd]
        on = jnp.concatenate(o_heads, axis=1) * nw_ref[...] * _sigmoid(g)
        o_ref[0, pl.ds(r0, HG_CHUNK), :] = x[r0:r0 + HG_CHUNK] + _dot(on.astype(BF16), wout_ref[...])

    for c in range(tc // HG_CHUNK):
        chunk_body(c)


def _hgrn_layer(x, mix_w, w_in, lb_logits, norm_w, w_out):
    bsz, seq, d = x.shape
    dv = norm_w.shape[0]
    nh = w_out.shape[0] // dv
    dk = (w_in.shape[1] - 2 * nh * dv) // (2 * nh)
    assert lb_logits.shape[0] == 2 and dk == LANES and dv == LANES
    tc = HG_TOKENS
    assert seq % tc == 0 and tc % HG_CHUNK == 0
    const = lambda b, j: (0, 0)
    return pl.pallas_call(
        functools.partial(_hgrn_kernel, nh=nh, dk=dk, dv=dv),
        grid=(bsz, seq // tc),
        in_specs=[
            pl.BlockSpec((1, tc, d), lambda b, j: (b, j, 0)),
            pl.BlockSpec((1, d), const),
            pl.BlockSpec(w_in.shape, const, pipeline_mode=pl.Buffered(1)),
            pl.BlockSpec(lb_logits.shape, const),
            pl.BlockSpec((1, nh * dv), const),
            pl.BlockSpec(w_out.shape, const, pipeline_mode=pl.Buffered(1)),
        ],
        out_specs=pl.BlockSpec((1, tc, d), lambda b, j: (b, j, 0)),
        out_shape=jax.ShapeDtypeStruct(x.shape, F32),
        scratch_shapes=[
            pltpu.VMEM((tc, w_in.shape[1]), F32),
            pltpu.VMEM((nh, dv, dk), F32),
        ],
        compiler_params=pltpu.CompilerParams(
            dimension_semantics=("arbitrary", "arbitrary"), vmem_limit_bytes=VMEM_LIMIT),
        name="hgrn_mixer",
    )(x, mix_w.reshape(1, d), w_in.astype(BF16), lb_logits, jnp.tile(norm_w, nh).reshape(1, nh * dv),
      w_out.astype(BF16))


def _swiglu_rows(h, wg_ref, wu_ref, wd_ref, acc):
    dff = wg_ref.shape[-1]
    for c0 in range(0, dff, FFN_CHUNK):
        g = _dot(h, wg_ref[:, c0:c0 + FFN_CHUNK])
        u = _dot(h, wu_ref[:, c0:c0 + FFN_CHUNK])
        a = (g * _sigmoid(g) * u).astype(BF16)
        y = _dot(a, wd_ref[c0:c0 + FFN_CHUNK, :])
        acc = y if acc is None else acc + y
    return acc


def _ffn_kernel(x_ref, nw_ref, wg_ref, wu_ref, wd_ref, o_ref):
    x = x_ref[...]
    h = _rms_rows(x, nw_ref[...]).astype(BF16)
    o_ref[...] = _swiglu_rows(h, wg_ref, wu_ref, wd_ref, x)


def _ffn_layer(x2, norm_w, w_gate, w_up, w_down):
    t, d = x2.shape
    dff = w_gate.shape[1]
    assert t % FFN_TOKENS == 0 and dff % FFN_CHUNK == 0
    const = lambda i: (0, 0)
    return pl.pallas_call(
        _ffn_kernel,
        grid=(t // FFN_TOKENS,),
        in_specs=[
            pl.BlockSpec((FFN_TOKENS, d), lambda i: (i, 0)),
            pl.BlockSpec((1, d), const),
            pl.BlockSpec((d, dff), const, pipeline_mode=pl.Buffered(1)),
            pl.BlockSpec((d, dff), const, pipeline_mode=pl.Buffered(1)),
            pl.BlockSpec((dff, d), const, pipeline_mode=pl.Buffered(1)),
        ],
        out_specs=pl.BlockSpec((FFN_TOKENS, d), lambda i: (i, 0)),
        out_shape=jax.ShapeDtypeStruct(x2.shape, F32),
        compiler_params=pltpu.CompilerParams(
            dimension_semantics=("arbitrary",), vmem_limit_bytes=VMEM_LIMIT),
        name="dense_ffn",
    )(x2, norm_w.reshape(1, d), w_gate.astype(BF16), w_up.astype(BF16), w_down.astype(BF16))


def _ssd_mix_block(proj_s, r0, side, x_ref, cw_ref, cb_ref, dtb_ref, alog_ref, dsk_ref, nw_ref, wout_ref, o_ref,
                   cbuf, y_s, st_s, *, d_inner):
    tc = SSD_BLOCK
    p2 = 2 * SSD_HEADDIM
    assert p2 == LANES and SSD_STATE == LANES
    gs = SSD_GROUPS * SSD_STATE
    cd = d_inner + 2 * gs
    pairs_per_group = d_inner // (SSD_GROUPS * p2)
    q = SSD_CHUNK
    assert q == LANES
    n_points = cd // LANES + (tc // q) * SSD_GROUPS * pairs_per_group
    emitted = [0, 0]

    def interleave():
        emitted[0] += 1
        while emitted[1] < len(side) and emitted[1] * n_points < emitted[0] * len(side):
            side[emitted[1]]()
            emitted[1] += 1

    for j in range(cd // LANES):
        c0 = d_inner + j * LANES
        cbuf[j, CONV_TAIL:CONV_TAIL + tc, :] = proj_s[:, c0:c0 + LANES]
        conv = cb_ref[:, j * LANES:(j + 1) * LANES]
        for k in range(CONV_W):
            off = CONV_TAIL - (CONV_W - 1) + k
            conv = conv + cw_ref[k:k + 1, j * LANES:(j + 1) * LANES] * cbuf[j, off:off + tc, :]
        cbuf[j, 0:CONV_TAIL, :] = cbuf[j, tc:tc + CONV_TAIL, :]
        proj_s[:, c0:c0 + LANES] = conv * _sigmoid(conv)
        interleave()

    dt_pre = proj_s[:, d_inner + cd:d_inner + cd + LANES] + dtb_ref[...]
    dt = jnp.maximum(dt_pre, 0.0) + jnp.log1p(jnp.exp(-jnp.abs(dt_pre)))
    da = dt * (-LOG2_E * jnp.exp(alog_ref[...]))

    tri = _tri_ones(q)
    rr = lax.broadcasted_iota(jnp.int32, (q, q), 0)
    cc = lax.broadcasted_iota(jnp.int32, (q, q), 1)
    causal = rr >= cc
    first = lax.broadcasted_iota(jnp.int32, (1, LANES), 1) < SSD_HEADDIM

    def lane_col(a, hd):
        return jnp.broadcast_to(a[:, hd:hd + 1], (q, LANES))

    for sc in range(tc // q):
        lo, hi = sc * q, (sc + 1) * q
        dtc = dt[lo:hi]
        cs = _cumsum_rows(tri, da[lo:hi])
        last = cs[q - 1:q, :]
        cs_t = cs.T
        for g in range(SSD_GROUPS):
            bm = proj_s[lo:hi, 2 * d_inner + g * SSD_STATE:2 * d_inner + (g + 1) * SSD_STATE]
            cm = proj_s[lo:hi, 2 * d_inner + gs + g * SSD_STATE:2 * d_inner + gs + (g + 1) * SSD_STATE]
            bmb = bm.astype(BF16)
            cmb = cm.astype(BF16)
            cb = _dot_nt(cmb, bmb)
            for pr in range(pairs_per_group):
                pi = g * pairs_per_group + pr
                h1, h2 = 2 * pi, 2 * pi + 1
                xp = proj_s[lo:hi, d_inner + pi * p2:d_inner + (pi + 1) * p2]
                c1 = lane_col(cs, h1)
                c2 = lane_col(cs, h2)
                cs_p = jnp.where(first, c1, c2)
                last_p = jnp.where(first, last[:, h1:h1 + 1], last[:, h2:h2 + 1])
                xdt = jnp.where(first, lane_col(dtc, h1), lane_col(dtc, h2)) * xp
                m1 = cb * jnp.exp2(jnp.where(causal, c1 - cs_t[h1:h1 + 1, :], -jnp.inf))
                m2 = cb * jnp.exp2(jnp.where(causal, c2 - cs_t[h2:h2 + 1, :], -jnp.inf))
                x1 = jnp.where(first, xdt, 0.0).astype(BF16)
                x2 = jnp.where(first, 0.0, xdt).astype(BF16)
                y = _dot(m1.astype(BF16), x1) + _dot(m2.astype(BF16), x2)
                st = st_s[pi]
                y = y + jnp.exp2(cs_p) * _dot(cmb, st.astype(BF16))
                wx = (jnp.exp2(last_p - cs_p) * xdt).astype(BF16)
                st_s[pi] = st * jnp.exp2(last_p) + _dot_tn(bmb, wx)
                y_s[lo:hi, pi * p2:(pi + 1) * p2] = y + dsk_ref[:, pi * p2:(pi + 1) * p2] * xp
                interleave()

    z = proj_s[:, 0:d_inner]
    y = y_s[...] * (z * _sigmoid(z))
    gw = d_inner // SSD_GROUPS
    parts = []
    for g in range(SSD_GROUPS):
        yg = y[:, g * gw:(g + 1) * gw]
        parts.append(yg * lax.rsqrt(jnp.mean(yg * yg, axis=-1, keepdims=True) + EPS))
    yn = jnp.concatenate(parts, axis=1) * nw_ref[...]
    o_ref[r0:r0 + tc, :] = x_ref[r0:r0 + tc, :] + _dot(yn.astype(BF16), wout_ref[...])


def _ssd_kernel(x_ref, xn_ref, mw_ref, win_ref, cw_ref, cb_ref, dtb_ref, alog_ref, dsk_ref, nw_ref, wout_ref, o_ref,
                proj_a, proj_b, cbuf, y_s, st_s, *, d_inner, steps_per_seq):
    tb = SSD_BLOCK
    n_all = win_ref.shape[1]
    step = pl.program_id(0)

    @pl.when(lax.rem(step, steps_per_seq) == 0)
    def _():
        st_s[...] = jnp.zeros_like(st_s)
        cbuf[:, 0:CONV_TAIL, :] = jnp.zeros((cbuf.shape[0], CONV_TAIL, LANES), F32)

    def projection_pieces(rows_ref, r0, dst):
        cache = []

        def piece(c0, c1):
            if not cache:
                cache.append(_rms_rows(rows_ref[r0:r0 + tb, :], mw_ref[...]).astype(BF16))
            dst[:, c0:c1] = _dot(cache[0], win_ref[:, c0:c1])

        return [functools.partial(piece, c0, min(c0 + SSD_PROJ_SLAB, n_all)) for c0 in range(0, n_all, SSD_PROJ_SLAB)]

    @pl.when(step == 0)
    def _():
        for piece in projection_pieces(x_ref, 0, proj_a):
            piece()

    mix = functools.partial(_ssd_mix_block, x_ref=x_ref, cw_ref=cw_ref, cb_ref=cb_ref, dtb_ref=dtb_ref,
                            alog_ref=alog_ref, dsk_ref=dsk_ref, nw_ref=nw_ref, wout_ref=wout_ref, o_ref=o_ref,
                            cbuf=cbuf, y_s=y_s, st_s=st_s, d_inner=d_inner)
    mix(proj_a, 0, projection_pieces(x_ref, tb, proj_b))
    mix(proj_b, tb, projection_pieces(xn_ref, 0, proj_a))


def _ssd_layer(x, mix_w, w_in, conv_w, conv_b, dt_bias, a_log, d_skip, norm_w, w_out):
    bsz, seq, d = x.shape
    d_inner = w_out.shape[0]
    nheads = dt_bias.shape[0]
    assert d_inner == nheads * SSD_HEADDIM and nheads <= LANES
    cd = d_inner + 2 * SSD_GROUPS * SSD_STATE
    assert conv_w.shape == (CONV_W, cd) and w_in.shape[1] == d_inner + cd + nheads
    tb = SSD_BLOCK
    assert seq % (2 * tb) == 0 and tb % SSD_CHUNK == 0
    t = bsz * seq
    pad = LANES - nheads
    w_all = jnp.pad(w_in, ((0, 0), (0, pad))).astype(BF16)
    n_all = w_all.shape[1]
    const = lambda i: (0, 0)
    last_block = t // tb - 1
    x2 = x.reshape(t, d)
    out = pl.pallas_call(
        functools.partial(_ssd_kernel, d_inner=d_inner, steps_per_seq=seq // (2 * tb)),
        grid=(t // (2 * tb),),
        in_specs=[
            pl.BlockSpec((2 * tb, d), lambda i: (i, 0)),
            pl.BlockSpec((tb, d), lambda i: (jnp.minimum(2 * i + 2, last_block), 0)),
            pl.BlockSpec((1, d), const),
            pl.BlockSpec(w_all.shape, const, pipeline_mode=pl.Buffered(1)),
            pl.BlockSpec((CONV_W, cd), const),
            pl.BlockSpec((1, cd), const),
            pl.BlockSpec((1, LANES), const),
            pl.BlockSpec((1, LANES), const),
            pl.BlockSpec((1, d_inner), const),
            pl.BlockSpec((1, d_inner), const),
            pl.BlockSpec(w_out.shape, const, pipeline_mode=pl.Buffered(1)),
        ],
        out_specs=pl.BlockSpec((2 * tb, d), lambda i: (i, 0)),
        out_shape=jax.ShapeDtypeStruct((t, d), F32),
        scratch_shapes=[
            pltpu.VMEM((tb, n_all), F32),
            pltpu.VMEM((tb, n_all), F32),
            pltpu.VMEM((cd // LANES, tb + CONV_TAIL, LANES), F32),
            pltpu.VMEM((tb, d_inner), F32),
            pltpu.VMEM((nheads // 2, SSD_STATE, LANES), F32),
        ],
        compiler_params=pltpu.CompilerParams(
            dimension_semantics=("arbitrary",), vmem_limit_bytes=VMEM_LIMIT),
        name="ssd_mixer",
    )(x2, x2, mix_w.reshape(1, d), w_all, conv_w, conv_b.reshape(1, cd),
      jnp.pad(dt_bias, (0, pad)).reshape(1, LANES), jnp.pad(a_log, (0, pad)).reshape(1, LANES),
      jnp.repeat(d_skip, SSD_HEADDIM).reshape(1, d_inner), norm_w.reshape(1, d_inner), w_out.astype(BF16))
    return out.reshape(bsz, seq, d)


def _router_kernel(x_ref, nw_ref, rw_ref, idx_ref, gate_ref):
    h = _rms_rows(x_ref[...], nw_ref[...])
    hi = h.astype(BF16)
    lo = (h - hi.astype(F32)).astype(BF16)
    both = _dot(hi, rw_ref[...])
    logits = both[:, :LANES] + both[:, LANES:] + _dot(lo, rw_ref[:, :LANES])
    lane = lax.broadcasted_iota(jnp.int32, logits.shape, 1)
    valid = lane < MOE_EXPERTS
    lg = jnp.where(valid, logits, -jnp.inf)
    e = jnp.exp(lg - jnp.max(lg, axis=-1, keepdims=True))
    p = jnp.where(valid, e / jnp.sum(e, axis=-1, keepdims=True), -1.0)
    m1 = jnp.max(p, axis=-1, keepdims=True)
    i1 = jnp.min(jnp.where(p == m1, lane, LANES), axis=-1, keepdims=True)
    p2 = jnp.where(lane == i1, -1.0, p)
    m2 = jnp.max(p2, axis=-1, keepdims=True)
    i2 = jnp.min(jnp.where(p2 == m2, lane, LANES), axis=-1, keepdims=True)
    s = m1 + m2
    idx_ref[...] = jnp.where(lane == 0, i1, jnp.where(lane == 1, i2, 0))
    gate_ref[...] = jnp.where(lane == 0, m1 / s, jnp.where(lane == 1, m2 / s, 0.0))


def _router(x2, norm_w, router_w):
    t, d = x2.shape
    ne = router_w.shape[1]
    assert ne == MOE_EXPERTS and t % ROUTER_TOKENS == 0
    rw = jnp.pad(router_w, ((0, 0), (0, LANES - ne)))
    rw_hi = rw.astype(BF16)
    rw_lo = (rw - rw_hi.astype(F32)).astype(BF16)
    rw = jnp.concatenate([rw_hi, rw_lo], axis=1)
    return pl.pallas_call(
        _router_kernel,
        grid=(t // ROUTER_TOKENS,),
        in_specs=[
            pl.BlockSpec((ROUTER_TOKENS, d), lambda i: (i, 0)),
            pl.BlockSpec((1, d), lambda i: (0, 0)),
            pl.BlockSpec((d, 2 * LANES), lambda i: (0, 0)),
        ],
        out_specs=[pl.BlockSpec((ROUTER_TOKENS, LANES), lambda i: (i, 0))] * 2,
        out_shape=[jax.ShapeDtypeStruct((t, LANES), jnp.int32), jax.ShapeDtypeStruct((t, LANES), F32)],
        compiler_params=pltpu.CompilerParams(dimension_semantics=("arbitrary",)),
        name="moe_router",
    )(x2, norm_w.reshape(1, d), rw)


def _for_binary_pieces(length, sizes, piece_fn, off=0):
    for size in sizes:
        take = (length & size) != 0

        @pl.when(take)
        def _(off=off, size=size):
            piece_fn(off, size)

        off = off + jnp.where(take, size, 0)


def _for_run_pieces(length, piece_fn):
    n_chunks = length // MOE_RUN_CHUNK

    def chunk(c, carry):
        piece_fn(c * MOE_RUN_CHUNK, MOE_RUN_CHUNK)
        return carry

    lax.fori_loop(0, n_chunks, chunk, 0)
    _for_binary_pieces(length, [s for s in MOE_RUN_PIECES if s < MOE_RUN_CHUNK], piece_fn, off=n_chunks * MOE_RUN_CHUNK)


def _aligned(x):
    return pl.multiple_of(x, MOE_RUN_ALIGN)


def _pair_one_hot(q_col, n):
    lane = lax.broadcasted_iota(jnp.int32, (q_col.shape[0], n), 1)
    return jnp.where(lane == q_col, 1.0, 0.0)


def _dispatch_kernel(run_ref, pad_ref, x_ref, q_ref, nw_ref, xs_hbm, stage, zbuf, sem, pad_sem):
    i = pl.program_id(0)
    last = pl.num_programs(0) - 1
    npair = stage.shape[0]
    ne = MOE_EXPERTS

    h = _rms_rows(x_ref[...], nw_ref[...]).astype(BF16)
    q = q_ref[...]
    sel = (_pair_one_hot(q[:, 0:1], npair) + _pair_one_hot(q[:, 1:2], npair)).astype(BF16)
    ordered = _dot_tn(sel, h)

    def start_block(blk):
        base = blk * MOE_RUN_FIELDS
        for e in range(ne):
            dst0 = run_ref[base + e]
            loc0 = run_ref[base + 2 * ne + e]

            def piece(off, size, dst0=dst0, loc0=loc0):
                pltpu.make_async_copy(stage.at[pl.ds(_aligned(loc0 + off), size), :],
                                      xs_hbm.at[pl.ds(_aligned(dst0 + off), size), :], sem).start()

            _for_run_pieces(run_ref[base + ne + e], piece)

    def wait_block(blk):
        def piece(off, size):
            pltpu.make_async_copy(stage.at[pl.ds(0, size), :], xs_hbm.at[pl.ds(0, size), :], sem).wait()

        _for_binary_pieces(run_ref[blk * MOE_RUN_FIELDS + 3 * ne], MOE_RUN_PIECES, piece)

    @pl.when(i > 0)
    def _():
        wait_block(i - 1)

    stage[...] = ordered
    start_block(i)

    @pl.when(i == last)
    def _():
        wait_block(i)
        zbuf[...] = jnp.zeros_like(zbuf)
        for e in range(ne):
            first = pad_ref[e]
            count = pad_ref[ne + e]

            def zero_issue(r, carry, first=first):
                pltpu.make_async_copy(zbuf.at[pl.ds(0, 1), :], xs_hbm.at[pl.ds(first + r, 1), :], pad_sem).start()
                return carry

            def zero_wait(r, carry, first=first):
                pltpu.make_async_copy(zbuf.at[pl.ds(0, 1), :], xs_hbm.at[pl.ds(first, 1), :], pad_sem).wait()
                return carry

            lax.fori_loop(0, count, zero_issue, 0)
            lax.fori_loop(0, count, zero_wait, 0)

        tm = zbuf.shape[0]
        first_tile = pad_ref[2 * ne]

        def tile_copy(j):
            return pltpu.make_async_copy(zbuf, xs_hbm.at[pl.ds((first_tile + j) * tm, tm), :], pad_sem)

        def tile_issue(j, carry):
            tile_copy(j).start()
            return carry

        def tile_wait(j, carry):
            tile_copy(j).wait()
            return carry

        n_unused = xs_hbm.shape[0] // tm - first_tile
        lax.fori_loop(0, n_unused, tile_issue, 0)
        lax.fori_loop(0, n_unused, tile_wait, 0)


def _dispatch(x2, norm_w, q2, run_info, pad_info, n_rows):
    t, d = x2.shape
    tb = COMBINE_TOKENS
    assert t % tb == 0
    grid_spec = pltpu.PrefetchScalarGridSpec(
        num_scalar_prefetch=2,
        grid=(t // tb,),
        in_specs=[
            pl.BlockSpec((tb, d), lambda i, run, pad: (i, 0)),
            pl.BlockSpec((tb, 2), lambda i, run, pad: (i, 0)),
            pl.BlockSpec((1, d), lambda i, run, pad: (0, 0)),
        ],
        out_specs=pl.BlockSpec(memory_space=pl.ANY),
        scratch_shapes=[pltpu.VMEM((MOE_STAGE_ROWS, d), F32), pltpu.VMEM((MOE_TILE, d), F32),
                        pltpu.SemaphoreType.DMA(()), pltpu.SemaphoreType.DMA(())],
    )
    return pl.pallas_call(
        _dispatch_kernel,
        grid_spec=grid_spec,
        out_shape=jax.ShapeDtypeStruct((n_rows, d), F32),
        compiler_params=pltpu.CompilerParams(dimension_semantics=("arbitrary",)),
        name="moe_dispatch",
    )(run_info, pad_info, x2, q2, norm_w.reshape(1, d))


def _experts_kernel(te_ref, tv_ref, ts_ref, xs_ref, wg_ref, wu_ref, wd_ref, y_ref):
    i = pl.program_id(0)

    @pl.when(tv_ref[i] > 0)
    def _():
        h = xs_ref[...].astype(BF16)
        y_ref[...] = _swiglu_rows(h, wg_ref.at[0], wu_ref.at[0], wd_ref.at[0], None)

    @pl.when(tv_ref[i] == 0)
    def _():
        y_ref[...] = jnp.zeros_like(y_ref)


def _experts(xs, tile_expert, tile_valid, tile_src, w_gate, w_up, w_down):
    n_rows, d = xs.shape
    ne, _, dff = w_gate.shape
    tm = MOE_TILE
    nt = n_rows // tm
    wmap = lambda i, te, tv, ts: (te[i], 0, 0)
    grid_spec = pltpu.PrefetchScalarGridSpec(
        num_scalar_prefetch=3,
        grid=(nt,),
        in_specs=[
            pl.BlockSpec((tm, d), lambda i, te, tv, ts: (ts[i], 0)),
            pl.BlockSpec((1, d, dff), wmap),
            pl.BlockSpec((1, d, dff), wmap),
            pl.BlockSpec((1, dff, d), wmap),
        ],
        out_specs=pl.BlockSpec((tm, d), lambda i, te, tv, ts: (i, 0)),
    )
    return pl.pallas_call(
        _experts_kernel,
        grid_spec=grid_spec,
        out_shape=jax.ShapeDtypeStruct((n_rows, d), F32),
        compiler_params=pltpu.CompilerParams(
            dimension_semantics=("arbitrary",), vmem_limit_bytes=VMEM_LIMIT),
        name="moe_experts",
    )(tile_expert, tile_valid, tile_src, xs, w_gate.astype(BF16), w_up.astype(BF16), w_down.astype(BF16))


def _combine_kernel(run_ref, x_ref, q_ref, gate_ref, fw_ref, y_hbm, o_ref, ybuf, sem):
    i = pl.program_id(0)
    npair = ybuf.shape[1]
    ne = MOE_EXPERTS
    slot = lax.rem(i, 2)

    def start_block(blk, s):
        tail = 2 * x_ref.shape[0]
        ybuf[s, tail:, :] = jnp.zeros((npair - tail, ybuf.shape[2]), F32)
        base = blk * MOE_RUN_FIELDS
        for e in range(ne):
            src0 = run_ref[base + e]
            loc0 = run_ref[base + 2 * ne + e]

            def piece(off, size, src0=src0, loc0=loc0):
                pltpu.make_async_copy(y_hbm.at[pl.ds(_aligned(src0 + off), size), :],
                                      ybuf.at[s, pl.ds(_aligned(loc0 + off), size), :], sem.at[s]).start()

            _for_run_pieces(run_ref[base + ne + e], piece)

    def wait_block(blk, s):
        def piece(off, size):
            pltpu.make_async_copy(y_hbm.at[pl.ds(0, size), :], ybuf.at[s, pl.ds(0, size), :], sem.at[s]).wait()

        _for_binary_pieces(run_ref[blk * MOE_RUN_FIELDS + 3 * ne], MOE_RUN_PIECES, piece)

    @pl.when(i == 0)
    def _():
        start_block(0, 0)

    @pl.when(i + 1 < pl.num_programs(0))
    def _():
        start_block(i + 1, 1 - slot)

    wait_block(i, slot)
    yb = ybuf[slot].astype(BF16)
    q = q_ref[...]
    g = gate_ref[...]
    y0 = _dot(_pair_one_hot(q[:, 0:1], npair).astype(BF16), yb)
    y1 = _dot(_pair_one_hot(q[:, 1:2], npair).astype(BF16), yb)
    o = x_ref[...] + g[:, 0:1] * y0 + g[:, 1:2] * y1
    o_ref[...] = _rms_rows(o, fw_ref[...])


def _combine(x2, final_w, q2, run_info, gate_pad, y_rows):
    t, d = x2.shape
    tb = COMBINE_TOKENS
    nb = t // tb
    grid_spec = pltpu.PrefetchScalarGridSpec(
        num_scalar_prefetch=1,
        grid=(nb,),
        in_specs=[
            pl.BlockSpec((tb, d), lambda i, run: (i, 0)),
            pl.BlockSpec((tb, 2), lambda i, run: (i, 0)),
            pl.BlockSpec((tb, LANES), lambda i, run: (i, 0)),
            pl.BlockSpec((1, d), lambda i, run: (0, 0)),
            pl.BlockSpec(memory_space=pl.ANY),
        ],
        out_specs=pl.BlockSpec((tb, d), lambda i, run: (i, 0)),
        scratch_shapes=[pltpu.VMEM((2, MOE_STAGE_ROWS, d), F32), pltpu.SemaphoreType.DMA((2,))],
    )
    return pl.pallas_call(
        _combine_kernel,
        grid_spec=grid_spec,
        out_shape=jax.ShapeDtypeStruct(x2.shape, F32),
        compiler_params=pltpu.CompilerParams(dimension_semantics=("arbitrary",)),
        name="moe_combine",
    )(run_info, x2, q2, gate_pad, final_w.reshape(1, d), y_rows)


def _moe_layer_and_final_norm(x2, norm_w, final_w, router_w, w_gate, w_up, w_down):
    t, d = x2.shape
    ne = MOE_EXPERTS
    tm = MOE_TILE
    tb = COMBINE_TOKENS
    nb = t // tb
    npair = 2 * tb
    idx_pad, gate_pad = _router(x2, norm_w, router_w)
    flat_e = idx_pad[:, :2].reshape(-1)
    onehot = (flat_e[:, None] == jnp.arange(ne, dtype=jnp.int32)[None, :]).astype(jnp.int32)
    csum = jnp.cumsum(onehot, axis=0)
    block_end = csum.reshape(nb, npair, ne)[:, -1, :]
    block_begin = jnp.concatenate([jnp.zeros((1, ne), jnp.int32), block_end[:-1]], axis=0)
    al = MOE_RUN_ALIGN
    run_len = ((block_end - block_begin + al - 1) // al) * al
    run_end = jnp.cumsum(run_len, axis=0)
    counts = run_end[-1]
    padded = ((counts + tm - 1) // tm) * tm
    ends = jnp.cumsum(padded)
    starts = ends - padded
    run_row = starts[None, :] + run_end - run_len
    run_loc = jnp.cumsum(run_len, axis=1) - run_len
    run_total = jnp.sum(run_len, axis=1, keepdims=True)
    run_info = jnp.concatenate([run_row, run_len, run_loc, run_total], axis=1).reshape(-1).astype(jnp.int32)
    local = (run_loc - block_begin)[:, None, :] + csum.reshape(nb, npair, ne) - 1
    q2 = jnp.sum(local * onehot.reshape(nb, npair, ne), axis=2).reshape(t, 2).astype(jnp.int32)
    nt = (2 * t + nb * ne * al) // tm + ne
    tile_ids = jnp.arange(nt, dtype=jnp.int32)
    tile_valid = (tile_ids * tm < ends[-1]).astype(jnp.int32)
    tile_expert = jnp.minimum(jnp.sum((tile_ids[:, None] * tm >= ends[None, :]).astype(jnp.int32), axis=1), ne - 1)
    tile_src = tile_ids * tile_valid
    pad_info = jnp.concatenate([starts + counts, padded - counts, ends[-1:] // tm]).astype(jnp.int32)
    xs = _dispatch(x2, norm_w, q2, run_info, pad_info, nt * tm)
    y_rows = _experts(xs, tile_expert.astype(jnp.int32), tile_valid, tile_src, w_gate, w_up, w_down)
    return _combine(x2, final_w, q2, run_info, gate_pad, y_rows)


def kernel(x, mix_norm_w, ffn_norm_w, final_norm_w, hg_w_in, hg_lb_logits, hg_norm_w, hg_w_out, ssd_w_in, ssd_conv_w, ssd_conv_b, ssd_dt_bias, ssd_a_log, ssd_d, ssd_norm_w, ssd_w_out, ffn_w_gate, ffn_w_up, ffn_w_down, moe_router, moe_w_gate, moe_w_up, moe_w_down):
    bsz, seq, d = x.shape
    assert mix_norm_w.shape[0] == 2 and hg_w_in.shape[0] == 1 and ssd_w_in.shape[0] == 1
    x = _hgrn_layer(x, mix_norm_w[0], hg_w_in[0], hg_lb_logits, hg_norm_w[0], hg_w_out[0])
    x2 = _ffn_layer(x.reshape(bsz * seq, d), ffn_norm_w[0], ffn_w_gate[0], ffn_w_up[0], ffn_w_down[0])
    x = _ssd_layer(x2.reshape(bsz, seq, d), mix_norm_w[1], ssd_w_in[0], ssd_conv_w[0], ssd_conv_b[0],
                   ssd_dt_bias[0], ssd_a_log[0], ssd_d[0], ssd_norm_w[0], ssd_w_out[0])
    out = _moe_layer_and_final_norm(x.reshape(bsz * seq, d), ffn_norm_w[1], final_norm_w, moe_router[0],
                                    moe_w_gate[0], moe_w_up[0], moe_w_down[0])
    return out.reshape(bsz, seq, d)
```
